```python
import math
import jax, jax.numpy as jnp
from jax import lax
import numpy as np

D_MODEL = 1024
BATCH = 8
SEQ = 2048
DEPTH = 2
DEC_BATCH = 8
DEC_SEQ = 8192
PAST_LEN = 128

N_ATTN_HEADS = 8
ATTN_HEAD_DIM = 64
ATTN_WIDTH = N_ATTN_HEADS * ATTN_HEAD_DIM
DILATED_PATTERNS = ((128, 1), (512, 4), (2048, 16))
ROPE_THETA = 500000.0
ROPE_DIM = ATTN_HEAD_DIM // 4
N_RET_HEADS = 4
RET_HEAD_DIM = 128
RET_WIDTH = N_RET_HEADS * RET_HEAD_DIM
RET_THETA = 10000.0
RET_CHUNK = 128
MIX_WIDTH = ATTN_WIDTH + RET_WIDTH
IN_COLS = 3 * ATTN_WIDTH + 4 * RET_WIDTH
D_FF = 2816
N_EXPERTS = 8
TOP_K = 2
D_FF_EXPERT = 2816
N_DENSE = (DEPTH + 1) // 2
N_MOE = DEPTH // 2
NORM_EPS = 1e-6
NEG_INF = -1e30
TINY = 1e-30

kernel_name = "hymba_dilated_retention_encoder"


def rms_norm(x, g):
    xf = x.astype(jnp.float32)
    y = xf * lax.rsqrt(jnp.mean(xf * xf, axis=-1, keepdims=True) + NORM_EPS)
    return (y * g.astype(jnp.float32)).astype(x.dtype)


def rotary(x, rot_dim, theta):
    s = x.shape[2]
    half = rot_dim // 2
    inv = theta ** (-jnp.arange(half, dtype=jnp.float32) * 2.0 / rot_dim)
    ang = jnp.arange(s, dtype=jnp.float32)[:, None] * inv[None, :]
    cos, sin = jnp.cos(ang), jnp.sin(ang)
    xf = x.astype(jnp.float32)
    x1, x2 = xf[..., :half], xf[..., half:rot_dim]
    out = jnp.concatenate([x1 * cos - x2 * sin, x2 * cos + x1 * sin, xf[..., rot_dim:]], axis=-1)
    return out.astype(x.dtype)


def dilated_branch(q, k, v, window, dilation):
    b, h, s, dh = q.shape
    radius = window // (2 * dilation)
    blk = radius
    L = s // dilation
    nb = -(-L // blk)
    Lp = nb * blk

    def classes(t):
        t = t.reshape(b, h, L, dilation, dh).transpose(0, 1, 3, 2, 4)
        return jnp.pad(t, ((0, 0), (0, 0), (0, 0), (0, Lp - L), (0, 0)))

    def windows(t):
        t = jnp.pad(classes(t), ((0, 0), (0, 0), (0, 0), (blk, blk), (0, 0)))
        t = t.reshape(b, h, dilation, nb + 2, blk, dh)
        return jnp.concatenate([t[:, :, :, :-2], t[:, :, :, 1:-1], t[:, :, :, 2:]], axis=4)

    qb = classes(q).reshape(b, h, dilation, nb, blk, dh)
    kw = windows(k)
    vw = windows(v)
    bi = jnp.arange(nb)[:, None]
    qpos = bi * blk + jnp.arange(blk)[None, :]
    kpos = (bi - 1) * blk + jnp.arange(3 * blk)[None, :]
    valid = ((jnp.abs(qpos[:, :, None] - kpos[:, None, :]) <= radius)
             & (kpos[:, None, :] >= 0) & (kpos[:, None, :] < L))
    sc = jnp.einsum("bhrnqd,bhrnkd->bhrnqk", qb, kw).astype(jnp.float32) * (dh ** -0.5)
    sc = jnp.where(valid, sc, NEG_INF)
    m = jnp.max(sc, axis=-1, keepdims=True)
    p = jnp.where(valid, jnp.exp(sc - m), 0.0)
    denom = jnp.maximum(jnp.sum(p, axis=-1), TINY)
    o = jnp.einsum("bhrnqk,bhrnkd->bhrnqd", p, vw.astype(jnp.float32)) / denom[..., None]
    lse = m[..., 0] + jnp.log(denom)
    o = o.reshape(b, h, dilation, Lp, dh)[:, :, :, :L].transpose(0, 1, 3, 2, 4).reshape(b, h, s, dh)
    lse = lse.reshape(b, h, dilation, Lp)[..., :L].transpose(0, 1, 3, 2).reshape(b, h, s)
    return o, lse


def dilated_attention(q, k, v):
    outs, lses = [], []
    for window, dilation in DILATED_PATTERNS:
        o, l = dilated_branch(q, k, v, window, dilation)
        outs.append(o)
        lses.append(l)
    w = jax.nn.softmax(jnp.stack(lses, axis=0), axis=0)
    return jnp.sum(w[..., None] * jnp.stack(outs, axis=0), axis=0)


def retention_dir(q, k, v, log_gamma, include_diag):
    b, h, s, dk = q.shape
    dv = v.shape[-1]
    n = s // RET_CHUNK
    qc = q.reshape(b, h, n, RET_CHUNK, dk).astype(jnp.float32)
    kc = k.reshape(b, h, n, RET_CHUNK, dk).astype(jnp.float32)
    vc = v.reshape(b, h, n, RET_CHUNK, dv).astype(jnp.float32)
    pos = jnp.arange(RET_CHUNK, dtype=jnp.float32)
    diff = pos[:, None] - pos[None, :]
    mask = (diff >= 0) if include_diag else (diff > 0)
    lg = log_gamma[:, None, None]
    decay = jnp.where(mask, jnp.exp(lg * jnp.where(mask, diff, 0.0)), 0.0)
    sc = jnp.einsum("bhncd,bhnmd->bhncm", qc, kc) * decay[None, :, None]
    intra = jnp.einsum("bhncm,bhnme->bhnce", sc, vc)
    q_decay = jnp.exp(log_gamma[:, None] * (pos + 1.0)[None, :])
    k_decay = jnp.exp(log_gamma[:, None] * (RET_CHUNK - 1.0 - pos)[None, :])
    chunk_decay = jnp.exp(log_gamma * RET_CHUNK)[None, :, None, None]
    kv = jnp.einsum("bhncd,hc,bhnce->nbhde", kc, k_decay, vc)

    def step(state, kv_i):
        return state * chunk_decay + kv_i, state

    _, prev = lax.scan(step, jnp.zeros((b, h, dk, dv), jnp.float32), kv)
    cross = jnp.einsum("bhncd,hc,nbhde->bhnce", qc, q_decay, prev)
    return (intra + cross).reshape(b, h, s, dv)


def bidir_retention(q, k, v, log_gamma_f, log_gamma_b):
    fwd = retention_dir(q, k, v, log_gamma_f, True)
    flip = lambda t: jnp.flip(t, axis=2)
    bwd = flip(retention_dir(flip(q), flip(k), flip(v), log_gamma_b, False))
    return fwd + bwd


def head_norm(y, g, centre):
    if centre:
        y = y - jnp.mean(y, axis=-1, keepdims=True)
    y = y * lax.rsqrt(jnp.mean(y * y, axis=-1, keepdims=True) + NORM_EPS)
    return y * g.astype(jnp.float32).reshape(y.shape[2], y.shape[3])


def token_mixer(h, w_in, attn_out_g, ret_out_g, decay_f, decay_b, w_out):
    b, s, _ = h.shape
    proj = h @ w_in
    cuts = [ATTN_WIDTH * i for i in (1, 2, 3)] + [3 * ATTN_WIDTH + RET_WIDTH * i for i in (1, 2, 3)]
    qa, ka, va, qr, kr, vr, gr = jnp.split(proj, cuts, axis=-1)
    heads = lambda t, nh: t.reshape(b, s, nh, -1).transpose(0, 2, 1, 3)
    qa = rotary(heads(qa, N_ATTN_HEADS), ROPE_DIM, ROPE_THETA)
    ka = rotary(heads(ka, N_ATTN_HEADS), ROPE_DIM, ROPE_THETA)
    att = dilated_attention(qa, ka, heads(va, N_ATTN_HEADS))
    att = head_norm(att.transpose(0, 2, 1, 3), attn_out_g, False).reshape(b, s, ATTN_WIDTH)
    qr = rotary(heads(qr, N_RET_HEADS), RET_HEAD_DIM, RET_THETA)
    kr = rotary(heads(kr, N_RET_HEADS), RET_HEAD_DIM, RET_THETA) * (RET_HEAD_DIM ** -0.5)
    lg_f = jnp.log1p(-jnp.exp(decay_f.astype(jnp.float32)))
    lg_b = jnp.log1p(-jnp.exp(decay_b.astype(jnp.float32)))
    ret = bidir_retention(qr, kr, heads(vr, N_RET_HEADS), lg_f, lg_b)
    ret = head_norm(ret.transpose(0, 2, 1, 3), ret_out_g, True).reshape(b, s, RET_WIDTH)
    ret = ret * jax.nn.silu(gr.astype(jnp.float32))
    mixed = jnp.concatenate([att.astype(h.dtype), ret.astype(h.dtype)], axis=-1)
    return mixed @ w_out


def swiglu(h, w1, w3, w2):
    return (jax.nn.silu(h @ w1) * (h @ w3)) @ w2


def moe_ffn(h, router, w1, w3, w2):
    logits = (h @ router).astype(jnp.float32)
    vals, idx = lax.top_k(logits, TOP_K)
    gates = jax.nn.softmax(vals, axis=-1)
    g = jnp.sum(jax.nn.one_hot(idx, N_EXPERTS, dtype=jnp.float32) * gates[..., None], axis=-2)
    out = jnp.zeros_like(h)
    for e in range(N_EXPERTS):
        out = out + g[..., e:e + 1].astype(h.dtype) * swiglu(h, w1[e], w3[e], w2[e])
    return out


def trunk(x, attn_norm_g, w_in, attn_out_g, ret_out_g, ret_decay_fwd, ret_decay_bwd, w_out,
          ffn_norm_g, dense_w1, dense_w3, dense_w2, moe_router, moe_w1, moe_w3, moe_w2, final_norm_g):
    for layer in range(DEPTH):
        h = rms_norm(x, attn_norm_g[layer])
        x = x + token_mixer(h, w_in[layer], attn_out_g[layer], ret_out_g[layer],
                            ret_decay_fwd[layer], ret_decay_bwd[layer], w_out[layer])
        h = rms_norm(x, ffn_norm_g[layer])
        i = layer // 2
        if layer % 2 == 0:
            x = x + swiglu(h, dense_w1[i], dense_w3[i], dense_w2[i])
        else:
            x = x + moe_ffn(h, moe_router[i], moe_w1[i], moe_w3[i], moe_w2[i])
    return rms_norm(x, final_norm_g)


def setup_inputs(seed: int = 0) -> dict:
    key = jax.random.key(seed)
    ks = jax.random.split(key, 20)
    nrm = lambda k, shape, scale: jax.random.normal(k, shape, jnp.float32) * scale
    gain = lambda k, shape: 1.0 + 0.01 * jax.random.normal(k, shape, jnp.float32)
    base_decay = -(5.0 + jnp.arange(N_RET_HEADS, dtype=jnp.float32)) * math.log(2.0)
    return {
        "x_prompt": nrm(ks[0], (BATCH, SEQ, D_MODEL), 1.0),
        "x_sample": nrm(ks[1], (DEC_BATCH, DEC_SEQ, D_MODEL), 1.0),
        "attn_norm_g": gain(ks[2], (DEPTH, D_MODEL)),
        "w_in": nrm(ks[3], (DEPTH, D_MODEL, IN_COLS), D_MODEL ** -0.5),
        "attn_out_g": gain(ks[4], (DEPTH, ATTN_WIDTH)),
        "ret_out_g": gain(ks[5], (DEPTH, RET_WIDTH)),
        "ret_decay_fwd": base_decay[None] + 0.1 * jax.random.normal(ks[6], (DEPTH, N_RET_HEADS), jnp.float32),
        "ret_decay_bwd": base_decay[None] + 0.1 * jax.random.normal(ks[7], (DEPTH, N_RET_HEADS), jnp.float32),
        "w_out": nrm(ks[8], (DEPTH, MIX_WIDTH, D_MODEL), MIX_WIDTH ** -0.5),
        "ffn_norm_g": gain(ks[9], (DEPTH, D_MODEL)),
        "dense_w1": nrm(ks[10], (N_DENSE, D_MODEL, D_FF), D_MODEL ** -0.5),
        "dense_w3": nrm(ks[11], (N_DENSE, D_MODEL, D_FF), D_MODEL ** -0.5),
        "dense_w2": nrm(ks[12], (N_DENSE, D_FF, D_MODEL), D_FF ** -0.5),
        "moe_router": nrm(ks[13], (N_MOE, D_MODEL, N_EXPERTS), D_MODEL ** -0.5),
        "moe_w1": nrm(ks[14], (N_MOE, N_EXPERTS, D_MODEL, D_FF_EXPERT), D_MODEL ** -0.5),
        "moe_w3": nrm(ks[15], (N_MOE, N_EXPERTS, D_MODEL, D_FF_EXPERT), D_MODEL ** -0.5),
        "moe_w2": nrm(ks[16], (N_MOE, N_EXPERTS, D_FF_EXPERT, D_MODEL), D_FF_EXPERT ** -0.5),
        "final_norm_g": gain(ks[17], (D_MODEL,)),
    }


def reference(x_prompt, x_sample, attn_norm_g, w_in, attn_out_g, ret_out_g, ret_decay_fwd,
              ret_decay_bwd, w_out, ffn_norm_g, dense_w1, dense_w3, dense_w2, moe_router,
              moe_w1, moe_w3, moe_w2, final_norm_g):
    y_prompt = trunk(x_prompt, attn_norm_g, w_in, attn_out_g, ret_out_g, ret_decay_fwd, ret_decay_bwd,
                     w_out, ffn_norm_g, dense_w1, dense_w3, dense_w2, moe_router, moe_w1, moe_w3,
                     moe_w2, final_norm_g)
    y_sample = trunk(x_sample, attn_norm_g, w_in, attn_out_g, ret_out_g, ret_decay_fwd, ret_decay_bwd,
                     w_out, ffn_norm_g, dense_w1, dense_w3, dense_w2, moe_router, moe_w1, moe_w3,
                     moe_w2, final_norm_g)
    return (y_prompt, y_sample)
```

```python
import functools
import math

import jax
import jax.numpy as jnp
from jax import lax
from jax.experimental import pallas as pl
from jax.experimental.pallas import tpu as pltpu

F32 = jnp.float32
BF16 = jnp.bfloat16

D_MODEL = 1024
N_ATTN_HEADS = 8
ATTN_HEAD_DIM = 64
ATTN_WIDTH = N_ATTN_HEADS * ATTN_HEAD_DIM
ATTN_RADIUS = 64
DILATIONS = (1, 4, 16)
ROPE_THETA = 500000.0
ROPE_DIM = ATTN_HEAD_DIM // 4
N_RET_HEADS = 4
RET_HEAD_DIM = 128
RET_WIDTH = N_RET_HEADS * RET_HEAD_DIM
RET_THETA = 10000.0
RET_CHUNK = 128
IN_COLS = 3 * ATTN_WIDTH + 4 * RET_WIDTH
N_EXPERTS = 8
D_FF = 2816
NORM_EPS = 1e-6
NEG_INF = -1e30
TINY = 1e-30

LANES = 128
N_COL_BLOCKS = IN_COLS // LANES
VMEM_LIMIT = 56 * 1024 * 1024

TM = 512
TME = 512
FF_CHUNKS = ((0, 1024), (1024, 2048), (2048, 2816))
TK = 1408
ATT_BQ = 128
ATT_W = ATT_BQ + 2 * ATTN_RADIUS


def _cparams(sem):
    return pltpu.CompilerParams(dimension_semantics=sem, vmem_limit_bytes=VMEM_LIMIT)


def _rms(x, g):
    return x * lax.rsqrt(jnp.mean(x * x, axis=-1, keepdims=True) + NORM_EPS) * g


def _silu(x):
    return x / (1.0 + jnp.exp(-x))


def _in_proj_body(x_ref, g_ref, w_ref, ca_ref, sa1_ref, sa2_ref, cr_ref, sr_ref, o_ref):
    h = _rms(x_ref[...], g_ref[...]).astype(BF16)
    for c in range(7):
        acc = jnp.dot(h, w_ref[:, c * 512:(c + 1) * 512], preferred_element_type=F32)
        for s in range(4):
            a = acc[:, s * LANES:(s + 1) * LANES]
            if c in (0, 1):
                a = (a * ca_ref[...] + pltpu.roll(a, ROPE_DIM // 2, 1) * sa1_ref[...]
                     + pltpu.roll(a, LANES - ROPE_DIM // 2, 1) * sa2_ref[...])
                if c == 0:
                    a = a * (ATTN_HEAD_DIM ** -0.5)
            elif c in (3, 4):
                a = a * cr_ref[...] + pltpu.roll(a, RET_HEAD_DIM // 2, 1) * sr_ref[...]
                if c == 4:
                    a = a * (RET_HEAD_DIM ** -0.5)
            o_ref[c * 4 + s] = a.astype(BF16)


def _in_proj(x2d, g, w, tabs, seq):
    t = x2d.shape[0]
    nblk = seq // TM
    tab_spec = pl.BlockSpec((TM, LANES), lambda i: (i % nblk, 0))
    return pl.pallas_call(
        _in_proj_body,
        grid=(t // TM,),
        in_specs=[pl.BlockSpec((TM, D_MODEL), lambda i: (i, 0)),
                  pl.BlockSpec((1, D_MODEL), lambda i: (0, 0)),
                  pl.BlockSpec((D_MODEL, IN_COLS), lambda i: (0, 0)),
                  tab_spec, tab_spec, tab_spec, tab_spec, tab_spec],
        out_specs=pl.BlockSpec((N_COL_BLOCKS, TM, LANES), lambda i: (0, i, 0)),
        out_shape=jax.ShapeDtypeStruct((N_COL_BLOCKS, t, LANES), BF16),
        compiler_params=_cparams(("arbitrary",)),
        name="in_proj",
    )(x2d, g, w, *tabs)


def _attn_body(q_ref, k_ref, v_ref, g_ref, o_ref,
               xf, q4, k4, v4, q16, k16, v16, acc, ms, ls, *, seq):
    stage = 512
    lane = lax.broadcasted_iota(jnp.int32, (ATT_BQ, LANES), 1)
    head0 = lane < ATTN_HEAD_DIM

    for src, d4, d16 in ((q_ref, q4, q16), (k_ref, k4, k16), (v_ref, v4, v16)):
        def upcast(i, c, src=src):
            r0 = pl.multiple_of(i * stage, stage)
            xf[pl.ds(r0, stage), :] = src[pl.ds(r0, stage), :].astype(F32)
            return c
        lax.fori_loop(0, seq // stage, upcast, 0)
        for d, dst in ((4, d4), (16, d16)):
            cls = seq // d
            ch = min(cls, 256)
            for r in range(d):
                def deinterleave(i, c, d=d, dst=dst, r=r, cls=cls, ch=ch):
                    i0 = pl.multiple_of(i * ch, ch)
                    dst[pl.ds(r * cls + i0, ch), :] = (
                        xf[pl.ds(r + d * i0, ch, stride=d), :].astype(BF16))
                    return c
                lax.fori_loop(0, cls // ch, deinterleave, 0)

    def branch(d, qd, kd, vd):
        cls = seq // d
        nblk = cls // ATT_BQ

        def block(t, c):
            base = pl.multiple_of(t * ATT_BQ, ATT_BQ)
            r = t // nblk
            n = t - r * nblk
            ws = pl.multiple_of(jnp.clip(base - ATTN_RADIUS, 0, seq - ATT_W), ATTN_RADIUS)
            qb = qd[pl.ds(base, ATT_BQ), :]
            kw = kd[pl.ds(ws, ATT_W), :]
            vw = vd[pl.ds(ws, ATT_W), :]
            qrow = base + lax.broadcasted_iota(jnp.int32, (ATT_BQ, ATT_W), 0)
            krow = ws + lax.broadcasted_iota(jnp.int32, (ATT_BQ, ATT_W), 1)
            valid = jnp.abs(krow - qrow) <= ATTN_RADIUS
            if d > 1:
                valid = valid & (krow >= r * cls) & (krow < (r + 1) * cls)
            o_h, m_h, l_h = [], [], []
            for hd in range(2):
                qh = jnp.where(head0 if hd == 0 else ~head0, qb, jnp.zeros_like(qb))
                s = lax.dot_general(qh, kw, (((1,), (1,)), ((), ())), preferred_element_type=F32)
                s = jnp.where(valid, s, NEG_INF)
                m = jnp.max(s, axis=1, keepdims=True)
                p = jnp.exp(s - m)
                l = jnp.maximum(jnp.sum(p, axis=1, keepdims=True), TINY)
                o_h.append(jnp.dot(p.astype(BF16), vw, preferred_element_type=F32))
                m_h.append(m)
                l_h.append(l)
            o_new = jnp.where(head0, o_h[0], o_h[1])
            m_new = jnp.where(head0, m_h[0], m_h[1])
            l_new = jnp.where(head0, l_h[0], l_h[1])
            if d == 1:
                rows = pl.ds(base, ATT_BQ)
                acc[rows, :] = o_new
                ms[rows, :] = m_new
                ls[rows, :] = l_new
            else:
                rows = pl.ds(d * (n * ATT_BQ) + r, ATT_BQ, stride=d)
                m_old = ms[rows, :]
                m_tot = jnp.maximum(m_old, m_new)
                a_old = jnp.exp(m_old - m_tot)
                a_new = jnp.exp(m_new - m_tot)
                acc[rows, :] = acc[rows, :] * a_old + o_new * a_new
                ls[rows, :] = ls[rows, :] * a_old + l_new * a_new
                ms[rows, :] = m_tot
            return c

        lax.fori_loop(0, seq // ATT_BQ, block, 0)

    branch(1, q_ref, k_ref, v_ref)
    branch(4, q4, k4, v4)
    branch(16, q16, k16, v16)

    gain = g_ref[...]

    def finish(i, c):
        rows = pl.ds(pl.multiple_of(i * ATT_BQ, ATT_BQ), ATT_BQ)
        y = acc[rows, :] / ls[rows, :]
        sq = y * y
        s0 = jnp.sum(jnp.where(head0, sq, 0.0), axis=1, keepdims=True)
        s1 = jnp.sum(jnp.where(head0, 0.0, sq), axis=1, keepdims=True)
        mean = jnp.where(head0, s0, s1) * (1.0 / ATTN_HEAD_DIM)
        o_ref[rows, :] = (y * lax.rsqrt(mean + NORM_EPS) * gain).astype(BF16)
        return c

    lax.fori_loop(0, seq // ATT_BQ, finish, 0)


def _attention(proj, gain, batch, seq):
    t = batch * seq
    qkv_spec = lambda off: pl.BlockSpec((None, seq, LANES), lambda b, hp: (off + hp, b, 0))
    return pl.pallas_call(
        functools.partial(_attn_body, seq=seq),
        grid=(batch, ATTN_WIDTH // LANES),
        in_specs=[qkv_spec(0), qkv_spec(4), qkv_spec(8),
                  pl.BlockSpec((None, 1, LANES), lambda b, hp: (hp, 0, 0))],
        out_specs=pl.BlockSpec((seq, LANES), lambda b, hp: (b, hp)),
        out_shape=jax.ShapeDtypeStruct((t, ATTN_WIDTH), BF16),
        scratch_shapes=[pltpu.VMEM((seq, LANES), F32)]
        + [pltpu.VMEM((seq, LANES), BF16)] * 6
        + [pltpu.VMEM((seq, LANES), F32)] * 3,
        compiler_params=_cparams(("arbitrary", "arbitrary")),
        name="dilated_attn",
    )(proj, proj, proj, gain)


def _ret_body(lg_ref, q_ref, k_ref, v_ref, gate_ref, gain_ref, o_ref, ob, dec, *, seq):
    c_len = RET_CHUNK
    n_chunks = seq // c_len
    hd = pl.program_id(1)
    lf = lg_ref[0, hd]
    lb = lg_ref[1, hd]
    ri = lax.broadcasted_iota(jnp.int32, (c_len, c_len), 0).astype(F32)
    ci = lax.broadcasted_iota(jnp.int32, (c_len, c_len), 1).astype(F32)
    diff = ri - ci
    dec[0] = jnp.where(diff >= 0, jnp.exp(lf * jnp.maximum(diff, 0.0)),
                       jnp.exp(lb * jnp.maximum(-diff, 0.0)))
    dec[1] = jnp.exp(lf * (ri + 1.0))
    dec[2] = jnp.exp(lf * (c_len - 1.0 - ri))
    dec[3] = jnp.exp(lb * (c_len - ri))
    dec[4] = jnp.exp(lb * ri)
    cdf = jnp.exp(lf * c_len)
    cdb = jnp.exp(lb * c_len)
    tdot = (((0,), (0,)), ((), ()))

    def bwd(j, state):
        rows = pl.ds(pl.multiple_of((n_chunks - 1 - j) * c_len, c_len), c_len)
        qn = q_ref[rows, :].astype(F32)
        kn = k_ref[rows, :].astype(F32)
        ob[rows, :] = jnp.dot((qn * dec[3]).astype(BF16), state.astype(BF16),
                              preferred_element_type=F32)
        kv = lax.dot_general((kn * dec[4]).astype(BF16), v_ref[rows, :], tdot,
                             preferred_element_type=F32)
        return state * cdb + kv

    lax.fori_loop(0, n_chunks, bwd, jnp.zeros((c_len, c_len), F32))

    gain = gain_ref[...]

    def fwd(j, state):
        rows = pl.ds(pl.multiple_of(j * c_len, c_len), c_len)
        qb = q_ref[rows, :]
        kb = k_ref[rows, :]
        vb = v_ref[rows, :]
        qn = qb.astype(F32)
        kn = kb.astype(F32)
        sc = lax.dot_general(qb, kb, (((1,), (1,)), ((), ())), preferred_element_type=F32) * dec[0]
        tot = (ob[rows, :] + jnp.dot(sc.astype(BF16), vb, preferred_element_type=F32)
               + jnp.dot((qn * dec[1]).astype(BF16), state.astype(BF16),
                         preferred_element_type=F32))
        y = tot - jnp.mean(tot, axis=-1, keepdims=True)
        y = y * lax.rsqrt(jnp.mean(y * y, axis=-1, keepdims=True) + NORM_EPS) * gain
        o_ref[rows, :] = (y * _silu(gate_ref[rows, :].astype(F32))).astype(BF16)
        kv = lax.dot_general((kn * dec[2]).astype(BF16), vb, tdot, preferred_element_type=F32)
        return state * cdf + kv

    lax.fori_loop(0, n_chunks, fwd, jnp.zeros((c_len, c_len), F32))


def _retention(proj, gain, lg, batch, seq):
    t = batch * seq
    spec = lambda off: pl.BlockSpec((None, seq, LANES), lambda b, hd: (off + hd, b, 0))
    return pl.pallas_call(
        functools.partial(_ret_body, seq=seq),
        grid=(batch, N_RET_HEADS),
        in_specs=[pl.BlockSpec(memory_space=pltpu.SMEM),
                  spec(12), spec(16), spec(20), spec(24),
                  pl.BlockSpec((None, 1, LANES), lambda b, hd: (hd, 0, 0))],
        out_specs=pl.BlockSpec((seq, LANES), lambda b, hd: (b, hd)),
        out_shape=jax.ShapeDtypeStruct((t, RET_WIDTH), BF16),
        scratch_shapes=[pltpu.VMEM((seq, LANES), F32),
                        pltpu.VMEM((5, RET_CHUNK, RET_CHUNK), F32)],
        compiler_params=_cparams(("arbitrary", "arbitrary")),
        name="retention",
    )(lg, proj, proj, proj, proj, gain)


def _post_dense_body(x_ref, a_ref, r_ref, wo_ref, g_ref, w1_ref, w3_ref, w2_ref, fg_ref, o_ref,
                     *, final_norm):
    x2 = (x_ref[...]
          + jnp.dot(a_ref[...], wo_ref[0:ATTN_WIDTH, :], preferred_element_type=F32)
          + jnp.dot(r_ref[...], wo_ref[ATTN_WIDTH:, :], preferred_element_type=F32))
    h = _rms(x2, g_ref[...]).astype(BF16)
    ffn = None
    for c0, c1 in FF_CHUNKS:
        a = jnp.dot(h, w1_ref[:, c0:c1], preferred_element_type=F32)
        b = jnp.dot(h, w3_ref[:, c0:c1], preferred_element_type=F32)
        u = (_silu(a) * b).astype(BF16)
        d = jnp.dot(u, w2_ref[c0:c1, :], preferred_element_type=F32)
        ffn = d if ffn is None else ffn + d
    y = x2 + ffn
    if final_norm:
        y = _rms(y, fg_ref[...])
    o_ref[...] = y


def _post_dense(x2d, att, ret, wo, g, w1, w3, w2, fg, final_norm):
    t = x2d.shape[0]
    full = lambda a: pl.BlockSpec(a.shape, lambda i: (0,) * a.ndim)
    return pl.pallas_call(
        functools.partial(_post_dense_body, final_norm=final_norm),
        grid=(t // TM,),
        in_specs=[pl.BlockSpec((TM, D_MODEL), lambda i: (i, 0)),
                  pl.BlockSpec((TM, ATTN_WIDTH), lambda i: (i, 0)),
                  pl.BlockSpec((TM, RET_WIDTH), lambda i: (i, 0)),
                  full(wo), full(g), full(w1), full(w3), full(w2), full(fg)],
        out_specs=pl.BlockSpec((TM, D_MODEL), lambda i: (i, 0)),
        out_shape=jax.ShapeDtypeStruct((t, D_MODEL), F32),
        compiler_params=_cparams(("arbitrary",)),
        name="post_dense",
    )(x2d, att, ret, wo, g, w1, w3, w2, fg)


def _post_route_body(x_ref, a_ref, r_ref, wo_ref, g_ref, rt_ref,
                     x2_ref, h_ref, ri_ref, rf_ref, cnt_ref, base):
    @pl.when(pl.program_id(0) == 0)
    def _():
        base[...] = jnp.zeros_like(base)

    x2 = (x_ref[...]
          + jnp.dot(a_ref[...], wo_ref[0:ATTN_WIDTH, :], preferred_element_type=F32)
          + jnp.dot(r_ref[...], wo_ref[ATTN_WIDTH:, :], preferred_element_type=F32))
    x2_ref[...] = x2
    h = _rms(x2, g_ref[...])
    h_ref[...] = h
    logits = jnp.dot(h, rt_ref[...], preferred_element_type=F32, precision=lax.Precision.HIGHEST)
    lane = lax.broadcasted_iota(jnp.int32, (TM, LANES), 1).astype(F32)
    lg = jnp.where(lane < N_EXPERTS, logits, -jnp.inf)
    m1 = jnp.max(lg, axis=1, keepdims=True)
    i1 = jnp.min(jnp.where(lg == m1, lane, float(LANES)), axis=1, keepdims=True)
    lg2 = jnp.where(lane == i1, -jnp.inf, lg)
    m2 = jnp.max(lg2, axis=1, keepdims=True)
    i2 = jnp.min(jnp.where(lg2 == m2, lane, float(LANES)), axis=1, keepdims=True)
    e = jnp.exp(m2 - m1)
    g1 = 1.0 / (1.0 + e)
    g2 = e / (1.0 + e)
    sel1 = lane == i1
    sel2 = lane == i2
    onehot = jnp.where(sel1 | sel2, 1.0, 0.0)
    row = lax.broadcasted_iota(jnp.int32, (TM, TM), 0)
    col = lax.broadcasted_iota(jnp.int32, (TM, TM), 1)
    earlier = jnp.where(col < row, 1.0, 0.0).astype(BF16)
    cnt = jnp.dot(earlier, onehot.astype(BF16), preferred_element_type=F32) + base[...]
    rank1 = jnp.sum(jnp.where(sel1, cnt, 0.0), axis=1, keepdims=True)
    rank2 = jnp.sum(jnp.where(sel2, cnt, 0.0), axis=1, keepdims=True)
    new_base = base[...] + jnp.sum(onehot, axis=0, keepdims=True)
    base[...] = new_base
    cnt_ref[...] = jnp.broadcast_to(new_base, cnt_ref.shape)
    ri = jnp.where(lane == 0, i1, jnp.where(lane == 1, i2,
                   jnp.where(lane == 2, rank1, jnp.where(lane == 3, rank2, 0.0))))
    ri_ref[...] = ri.astype(jnp.int32)
    rf_ref[...] = jnp.where(lane == 0, g1, jnp.where(lane == 1, g2, 0.0))


def _post_route(x2d, att, ret, wo, g, router_pad):
    t = x2d.shape[0]
    full = lambda a: pl.BlockSpec(a.shape, lambda i: (0,) * a.ndim)
    row = lambda w: pl.BlockSpec((TM, w), lambda i: (i, 0))
    return pl.pallas_call(
        _post_route_body,
        grid=(t // TM,),
        in_specs=[row(D_MODEL), row(ATTN_WIDTH), row(RET_WIDTH),
                  full(wo), full(g), full(router_pad)],
        out_specs=[row(D_MODEL), row(D_MODEL), row(LANES), row(LANES),
                   pl.BlockSpec((8, LANES), lambda i: (0, 0))],
        out_shape=[jax.ShapeDtypeStruct((t, D_MODEL), F32),
                   jax.ShapeDtypeStruct((t, D_MODEL), F32),
                   jax.ShapeDtypeStruct((t, LANES), jnp.int32),
                   jax.ShapeDtypeStruct((t, LANES), F32),
                   jax.ShapeDtypeStruct((8, LANES), F32)],
        scratch_shapes=[pltpu.VMEM((1, LANES), F32)],
        compiler_params=_cparams(("arbitrary",)),
        name="post_route",
    )(x2d, att, ret, wo, g, router_pad)


def _row_copy_wait(src, dst, sem):
    pltpu.make_async_copy(src, dst, sem).wait()


def _scatter_body(ginfo, pos_hbm, h_ref, o_hbm, idx, zbuf, sem_i, sem_r, sem_z, *, n_sorted):
    i = pl.program_id(0)

    @pl.when(i == 0)
    def _():
        zbuf[...] = jnp.zeros_like(zbuf)
        total = ginfo[N_EXPERTS - 1]
        fills = []
        for e in range(N_EXPERTS):
            start = pl.multiple_of(jnp.maximum(ginfo[e] - TME, 0), TME)
            fills.append((ginfo[N_EXPERTS + e] > 0, start))
        for j in range(N_EXPERTS):
            start = n_sorted - (j + 1) * TME
            fills.append((start >= total, start))
        for phase in ("start", "wait"):
            for cond, start in fills:
                @pl.when(cond)
                def _(start=start, phase=phase):
                    cp = pltpu.make_async_copy(zbuf, o_hbm.at[pl.ds(start, TME), :], sem_z)
                    cp.start() if phase == "start" else cp.wait()

    cp = pltpu.make_async_copy(pos_hbm.at[i], idx, sem_i)
    cp.start()
    cp.wait()

    def issue(t, c):
        src = h_ref.at[pl.ds(t, 1), :]
        pltpu.make_async_copy(src, o_hbm.at[pl.ds(idx[t], 1), :], sem_r).start()
        pltpu.make_async_copy(src, o_hbm.at[pl.ds(idx[TM + t], 1), :], sem_r).start()
        return c

    lax.fori_loop(0, TM, issue, 0)

    def drain(t, c):
        src = h_ref.at[pl.ds(t, 1), :]
        _row_copy_wait(src, o_hbm.at[pl.ds(idx[t], 1), :], sem_r)
        _row_copy_wait(src, o_hbm.at[pl.ds(idx[TM + t], 1), :], sem_r)
        return c

    lax.fori_loop(0, TM, drain, 0)


def _scatter(ginfo, pos, h, n_sorted):
    t = h.shape[0]
    return pl.pallas_call(
        functools.partial(_scatter_body, n_sorted=n_sorted),
        grid=(t // TM,),
        in_specs=[pl.BlockSpec(memory_space=pltpu.SMEM),
                  pl.BlockSpec(memory_space=pl.ANY),
                  pl.BlockSpec((TM, D_MODEL), lambda i: (i, 0))],
        out_specs=pl.BlockSpec(memory_space=pl.ANY),
        out_shape=jax.ShapeDtypeStruct((n_sorted, D_MODEL), F32),
        scratch_shapes=[pltpu.SMEM((2 * TM,), jnp.int32),
                        pltpu.VMEM((TME, D_MODEL), F32),
                        pltpu.SemaphoreType.DMA, pltpu.SemaphoreType.DMA,
                        pltpu.SemaphoreType.DMA],
        compiler_params=_cparams(("arbitrary",)),
        name="moe_dispatch",
    )(ginfo, pos, h)


def _expert_body(te_ref, nv_ref, h_ref, w1_ref, w3_ref, w2_ref, o_ref, hb):
    j = pl.program_id(0)
    k = pl.program_id(1)
    nv = nv_ref[j]

    @pl.when((nv == 0) & (k == 0))
    def _():
        o_ref[...] = jnp.zeros_like(o_ref)

    @pl.when(nv > 0)
    def _():
        @pl.when(k == 0)
        def _():
            rows = lax.broadcasted_iota(jnp.int32, (TME, D_MODEL), 0)
            hb[...] = jnp.where(rows < nv, h_ref[...], 0.0).astype(BF16)
            o_ref[...] = jnp.zeros_like(o_ref)

        h = hb[...]
        a = jnp.dot(h, w1_ref[...], preferred_element_type=F32)
        b = jnp.dot(h, w3_ref[...], preferred_element_type=F32)
        u = (_silu(a) * b).astype(BF16)
        o_ref[...] += jnp.dot(u, w2_ref[...], preferred_element_type=F32)


def _experts(tile_e, tile_nv, hs, w1, w3, w2):
    n_sorted = hs.shape[0]
    grid_spec = pltpu.PrefetchScalarGridSpec(
        num_scalar_prefetch=2,
        grid=(n_sorted // TME, D_FF // TK),
        in_specs=[pl.BlockSpec((TME, D_MODEL), lambda j, k, te, nv: (j, 0)),
                  pl.BlockSpec((None, D_MODEL, TK), lambda j, k, te, nv: (te[j], 0, k)),
                  pl.BlockSpec((None, D_MODEL, TK), lambda j, k, te, nv: (te[j], 0, k)),
                  pl.BlockSpec((None, TK, D_MODEL), lambda j, k, te, nv: (te[j], k, 0))],
        out_specs=pl.BlockSpec((TME, D_MODEL), lambda j, k, te, nv: (j, 0)),
        scratch_shapes=[pltpu.VMEM((TME, D_MODEL), BF16)],
    )
    return pl.pallas_call(
        _expert_body,
        grid_spec=grid_spec,
        out_shape=jax.ShapeDtypeStruct((n_sorted, D_MODEL), F32),
        compiler_params=_cparams(("arbitrary", "arbitrary")),
        name="moe_experts",
    )(tile_e, tile_nv, hs, w1, w3, w2)


def _combine_body(pos_hbm, x_ref, rf_ref, fg_ref, y_hbm, o_ref, idx, b1, b2, sem_i, sem_r,
                  *, final_norm):
    i = pl.program_id(0)
    cp = pltpu.make_async_copy(pos_hbm.at[i], idx, sem_i)
    cp.start()
    cp.wait()

    def issue(t, c):
        pltpu.make_async_copy(y_hbm.at[pl.ds(idx[t], 1), :], b1.at[pl.ds(t, 1), :], sem_r).start()
        pltpu.make_async_copy(y_hbm.at[pl.ds(idx[TM + t], 1), :], b2.at[pl.ds(t, 1), :], sem_r).start()
        return c

    lax.fori_loop(0, TM, issue, 0)

    def drain(t, c):
        _row_copy_wait(y_hbm.at[pl.ds(idx[t], 1), :], b1.at[pl.ds(t, 1), :], sem_r)
        _row_copy_wait(y_hbm.at[pl.ds(idx[TM + t], 1), :], b2.at[pl.ds(t, 1), :], sem_r)
        return c

    lax.fori_loop(0, TM, drain, 0)

    g1 = rf_ref[:, 0:1]
    g2 = rf_ref[:, 1:2]
    y = x_ref[...] + (g1 * b1[...] + g2 * b2[...])
    if final_norm:
        y = _rms(y, fg_ref[...])
    o_ref[...] = y


def _combine(pos, x2, rf, fg, ys, final_norm):
    t = x2.shape[0]
    return pl.pallas_call(
        functools.partial(_combine_body, final_norm=final_norm),
        grid=(t // TM,),
        in_specs=[pl.BlockSpec(memory_space=pl.ANY),
                  pl.BlockSpec((TM, D_MODEL), lambda i: (i, 0)),
                  pl.BlockSpec((TM, LANES), lambda i: (i, 0)),
                  pl.BlockSpec((1, D_MODEL), lambda i: (0, 0)),
                  pl.BlockSpec(memory_space=pl.ANY)],
        out_specs=pl.BlockSpec((TM, D_MODEL), lambda i: (i, 0)),
        out_shape=jax.ShapeDtypeStruct((t, D_MODEL), F32),
        scratch_shapes=[pltpu.SMEM((2 * TM,), jnp.int32),
                        pltpu.VMEM((TM, D_MODEL), F32), pltpu.VMEM((TM, D_MODEL), F32),
                        pltpu.SemaphoreType.DMA, pltpu.SemaphoreType.DMA],
        compiler_params=_cparams(("arbitrary",)),
        name="moe_combine",
    )(pos, x2, rf, fg, ys)


def _moe(x2d, att, ret, wo, g, router_pad, w1, w3, w2, fg, final_norm):
    t = x2d.shape[0]
    n_sorted = 2 * t + N_EXPERTS * TME
    n_tiles = n_sorted // TME
    x2, h, ri, rf, cnt = _post_route(x2d, att, ret, wo, g, router_pad)

    counts = cnt[0, :N_EXPERTS].astype(jnp.int32)
    padded = ((counts + TME - 1) // TME) * TME
    ends = jnp.cumsum(padded)
    starts = ends - padded
    experts = jnp.arange(N_EXPERTS, dtype=jnp.int32)
    start_of = lambda e: jnp.sum(jnp.where(e[:, None] == experts[None, :], starts[None, :], 0), axis=1)
    pos1 = start_of(ri[:, 0]) + ri[:, 2]
    pos2 = start_of(ri[:, 1]) + ri[:, 3]
    pos = jnp.concatenate([pos1.reshape(t // TM, TM), pos2.reshape(t // TM, TM)], axis=1)
    tile_start = jnp.arange(n_tiles, dtype=jnp.int32) * TME
    tile_e = jnp.minimum(jnp.sum(tile_start[:, None] >= ends[None, :], axis=1), N_EXPERTS - 1)
    tile_e = tile_e.astype(jnp.int32)
    tile_nv = jnp.clip(starts[tile_e] + counts[tile_e] - tile_start, 0, TME)
    tile_nv = jnp.where(tile_start < ends[-1], tile_nv, 0).astype(jnp.int32)
    ginfo = jnp.concatenate([ends, padded]).astype(jnp.int32)

    hs = _scatter(ginfo, pos, h, n_sorted)
    ys = _experts(tile_e, tile_nv, hs, w1, w3, w2)
    return _combine(pos, x2, rf, fg, ys, final_norm)


def _rope_tables(seq):
    pos = jnp.arange(seq, dtype=F32)[:, None]
    lane = jnp.arange(LANES)
    j = lane % ATTN_HEAD_DIM
    half = ROPE_DIM // 2
    inv_a = ROPE_THETA ** (-jnp.arange(half, dtype=F32) * 2.0 / ROPE_DIM)
    ang = pos * inv_a[None, :]
    cos_a, sin_a = jnp.cos(ang), jnp.sin(ang)
    jc = jnp.where(j < half, j, j - half) % half
    ca = jnp.where(j[None, :] < ROPE_DIM, cos_a[:, jc], 1.0)
    sa1 = jnp.where(((j >= half) & (j < ROPE_DIM))[None, :], sin_a[:, jc], 0.0)
    sa2 = jnp.where((j < half)[None, :], -sin_a[:, jc], 0.0)
    half_r = RET_HEAD_DIM // 2
    inv_r = RET_THETA ** (-jnp.arange(half_r, dtype=F32) * 2.0 / RET_HEAD_DIM)
    ang_r = pos * inv_r[None, :]
    cos_r, sin_r = jnp.cos(ang_r), jnp.sin(ang_r)
    cr = cos_r[:, lane % half_r]
    sr = jnp.where((lane < half_r)[None, :], -sin_r[:, lane % half_r], sin_r[:, lane % half_r])
    return ca, sa1, sa2, cr, sr


def _trunk(x, w):
    batch, seq, _ = x.shape
    t = batch * seq
    assert seq % (16 * ATT_BQ) == 0 and seq % TM == 0 and t % TM == 0
    tabs = _rope_tables(seq)
    x2d = x.reshape(t, D_MODEL)
    depth = w["w_in"].shape[0]
    for layer in range(depth):
        last = layer == depth - 1
        proj = _in_proj(x2d, w["attn_norm_g"][layer][None, :], w["w_in"][layer], tabs, seq)
        att = _attention(proj, w["attn_out_g"][layer].reshape(ATTN_WIDTH // LANES, 1, LANES), batch, seq)
        ret = _retention(proj, w["ret_out_g"][layer].reshape(N_RET_HEADS, 1, LANES),
                         w["lg"][layer], batch, seq)
        i = layer // 2
        if layer % 2 == 0:
            x2d = _post_dense(x2d, att, ret, w["w_out"][layer], w["ffn_norm_g"][layer][None, :],
                              w["dense_w1"][i], w["dense_w3"][i], w["dense_w2"][i],
                              w["final_norm_g"], last)
        else:
            x2d = _moe(x2d, att, ret, w["w_out"][layer], w["ffn_norm_g"][layer][None, :],
                       w["router_pad"][i], w["moe_w1"][i], w["moe_w3"][i], w["moe_w2"][i],
                       w["final_norm_g"], last)
    return x2d.reshape(batch, seq, D_MODEL)


def kernel(x_prompt, x_sample, attn_norm_g, w_in, attn_out_g, ret_out_g, ret_decay_fwd, ret_decay_bwd, w_out, ffn_norm_g, dense_w1, dense_w3, dense_w2, moe_router, moe_w1, moe_w3, moe_w2, final_norm_g):
    lg_f = jnp.log1p(-jnp.exp(ret_decay_fwd.astype(F32)))
    lg_b = jnp.log1p(-jnp.exp(ret_decay_bwd.astype(F32)))
    w = {
        "attn_norm_g": attn_norm_g, "attn_out_g": attn_out_g, "ret_out_g": ret_out_g,
        "ffn_norm_g": ffn_norm_g, "final_norm_g": final_norm_g[None, :],
        "lg": jnp.stack([lg_f, lg_b], axis=1),
        "w_in": w_in.astype(BF16), "w_out": w_out.astype(BF16),
        "dense_w1": dense_w1.astype(BF16), "dense_w3": dense_w3.astype(BF16),
        "dense_w2": dense_w2.astype(BF16),
        "moe_w1": moe_w1.astype(BF16), "moe_w3": moe_w3.astype(BF16), "moe_w2": moe_w2.astype(BF16),
        "router_pad": jnp.pad(moe_router.astype(F32), ((0, 0), (0, 0), (0, LANES - N_EXPERTS))),
    }
    return _trunk(x_prompt, w), _trunk(x_sample, w)
```

```python
import functools
import math

import jax
import jax.numpy as jnp
from jax import lax
from jax.experimental import pallas as pl
from jax.experimental.pallas import tpu as pltpu

F32 = jnp.float32
BF16 = jnp.bfloat16

D_MODEL = 1024
N_ATTN_HEADS = 8
ATTN_HEAD_DIM = 64
ATTN_WIDTH = N_ATTN_HEADS * ATTN_HEAD_DIM
ATTN_RADIUS = 64
DILATIONS = (1, 4, 16)
ROPE_THETA = 500000.0
ROPE_DIM = ATTN_HEAD_DIM // 4
N_RET_HEADS = 4
RET_HEAD_DIM = 128
RET_WIDTH = N_RET_HEADS * RET_HEAD_DIM
RET_THETA = 10000.0
RET_CHUNK = 128
IN_COLS = 3 * ATTN_WIDTH + 4 * RET_WIDTH
N_EXPERTS = 8
D_FF = 2816
NORM_EPS = 1e-6
NEG_INF = -1e30
TINY = 1e-30

LANES = 128
N_COL_BLOCKS = IN_COLS // LANES
VMEM_LIMIT = 56 * 1024 * 1024

TM = 512
TME = 512
FF_CHUNKS = ((0, 1024), (1024, 2048), (2048, 2816))
TK = 1408
ATT_BQ = 128
ATT_W = ATT_BQ + 2 * ATTN_RADIUS
ATT_NB = 8
RET_UNROLL = 4
DMA_UNROLL = 8


def _cparams(sem):
    return pltpu.CompilerParams(dimension_semantics=sem, vmem_limit_bytes=VMEM_LIMIT)


def _rms(x, g):
    return x * lax.rsqrt(jnp.mean(x * x, axis=-1, keepdims=True) + NORM_EPS) * g


def _silu(x):
    return x / (1.0 + jnp.exp(-x))


def _in_proj_body(x_ref, g_ref, w_ref, ca_ref, sa1_ref, sa2_ref, cr_ref, sr_ref, o_ref):
    h = _rms(x_ref[...], g_ref[...]).astype(BF16)
    for c in range(7):
        acc = jnp.dot(h, w_ref[:, c * 512:(c + 1) * 512], preferred_element_type=F32)
        for s in range(4):
            a = acc[:, s * LANES:(s + 1) * LANES]
            if c in (0, 1):
                a = (a * ca_ref[...] + pltpu.roll(a, ROPE_DIM // 2, 1) * sa1_ref[...]
                     + pltpu.roll(a, LANES - ROPE_DIM // 2, 1) * sa2_ref[...])
                if c == 0:
                    a = a * (ATTN_HEAD_DIM ** -0.5)
            elif c in (3, 4):
                a = a * cr_ref[...] + pltpu.roll(a, RET_HEAD_DIM // 2, 1) * sr_ref[...]
                if c == 4:
                    a = a * (RET_HEAD_DIM ** -0.5)
            o_ref[c * 4 + s] = a.astype(BF16)


def _in_proj(x2d, g, w, tabs, seq):
    t = x2d.shape[0]
    nblk = seq // TM
    tab_spec = pl.BlockSpec((TM, LANES), lambda i: (i % nblk, 0))
    return pl.pallas_call(
        _in_proj_body,
        grid=(t // TM,),
        in_specs=[pl.BlockSpec((TM, D_MODEL), lambda i: (i, 0)),
                  pl.BlockSpec((1, D_MODEL), lambda i: (0, 0)),
                  pl.BlockSpec((D_MODEL, IN_COLS), lambda i: (0, 0)),
                  tab_spec, tab_spec, tab_spec, tab_spec, tab_spec],
        out_specs=pl.BlockSpec((N_COL_BLOCKS, TM, LANES), lambda i: (0, i, 0)),
        out_shape=jax.ShapeDtypeStruct((N_COL_BLOCKS, t, LANES), BF16),
        compiler_params=_cparams(("arbitrary",)),
        name="in_proj",
    )(x2d, g, w, *tabs)


def _attn_body(q_ref, k_ref, v_ref, g_ref, o_ref,
               xf, k1, v1, q4, k4, v4, q16, k16, v16, bias, acc, ms, ls, *, seq):
    stage = 512
    pad = ATTN_RADIUS
    lane = lax.broadcasted_iota(jnp.int32, (ATT_BQ, LANES), 1)
    head0 = lane < ATTN_HEAD_DIM
    khead0 = lax.broadcasted_iota(jnp.int32, (ATT_W, LANES), 1) < ATTN_HEAD_DIM

    qi = lax.broadcasted_iota(jnp.int32, (ATT_BQ, ATT_W), 0)
    kj = lax.broadcasted_iota(jnp.int32, (ATT_BQ, ATT_W), 1)
    band = jnp.abs(kj - pad - qi) <= ATTN_RADIUS
    for e in range(4):
        ok = band
        if e & 1:
            ok = ok & (kj >= pad)
        if e & 2:
            ok = ok & (kj < ATT_BQ + pad)
        bias[e] = jnp.where(ok, 0.0, NEG_INF)

    for arr in (k1, v1, k4, v4, k16, v16):
        arr[pl.ds(0, pad), :] = jnp.zeros((pad, LANES), BF16)
        arr[pl.ds(seq + pad, pad), :] = jnp.zeros((pad, LANES), BF16)

    for src, d1, d4, d16 in ((q_ref, None, q4, q16), (k_ref, k1, k4, k16), (v_ref, v1, v4, v16)):
        off = 0 if d1 is None else pad

        def upcast(i, c, src=src, d1=d1):
            r0 = pl.multiple_of(i * stage, stage)
            x = src[pl.ds(r0, stage), :]
            xf[pl.ds(r0, stage), :] = x.astype(F32)
            if d1 is not None:
                d1[pl.ds(pad + r0, stage), :] = x
            return c
        lax.fori_loop(0, seq // stage, upcast, 0)
        for d, dst in ((4, d4), (16, d16)):
            cls = seq // d
            ch = min(cls, 256)
            for r in range(d):
                def deinterleave(i, c, d=d, dst=dst, r=r, cls=cls, ch=ch, off=off):
                    i0 = pl.multiple_of(i * ch, ch)
                    dst[pl.ds(off + r * cls + i0, ch), :] = (
                        xf[pl.ds(r + d * i0, ch, stride=d), :].astype(BF16))
                    return c
                lax.fori_loop(0, cls // ch, deinterleave, 0)

    def branch(d, qd, kd, vd):
        cls = seq // d
        nblk = cls // ATT_BQ
        shift = nblk.bit_length() - 1

        def one_block(t):
            base = pl.multiple_of(t * ATT_BQ, ATT_BQ)
            n = t & (nblk - 1)
            r = t >> shift
            edge = jnp.where(n == 0, 1, 0) + jnp.where(n == nblk - 1, 2, 0)
            b = bias[edge]
            qb = qd[pl.ds(base, ATT_BQ), :]
            kw = kd[pl.ds(base, ATT_W), :]
            vw = vd[pl.ds(base, ATT_W), :]
            full, m_h = [], []
            for hd in range(2):
                qh = jnp.where(head0 if hd == 0 else ~head0, qb, jnp.zeros_like(qb))
                s = lax.dot_general(qh, kw, (((1,), (1,)), ((), ())), preferred_element_type=F32) + b
                m = jnp.max(s, axis=1, keepdims=True)
                p = jnp.exp(s - m).astype(BF16)
                vh = jnp.where(khead0 if hd == 0 else ~khead0, vw, jnp.ones_like(vw))
                full.append(jnp.dot(p, vh, preferred_element_type=F32))
                m_h.append(m)
            o_new = jnp.where(head0, full[0], full[1])
            l_new = pltpu.roll(jnp.where(head0, full[1], full[0]), ATTN_HEAD_DIM, 1)
            l_new = jnp.maximum(l_new, TINY)
            m_new = jnp.where(head0, m_h[0], m_h[1])
            if d == 1:
                rows = pl.ds(base, ATT_BQ)
                acc[rows, :] = o_new
                ms[rows, :] = m_new
                ls[rows, :] = l_new
            else:
                rows = pl.ds(d * (n * ATT_BQ) + r, ATT_BQ, stride=d)
                m_old = ms[rows, :]
                m_tot = jnp.maximum(m_old, m_new)
                a_old = jnp.exp(m_old - m_tot)
                a_new = jnp.exp(m_new - m_tot)
                acc[rows, :] = acc[rows, :] * a_old + o_new * a_new
                ls[rows, :] = ls[rows, :] * a_old + l_new * a_new
                ms[rows, :] = m_tot

        def group(g, c):
            for u in range(ATT_NB):
                one_block(g * ATT_NB + u)
            return c

        lax.fori_loop(0, seq // (ATT_BQ * ATT_NB), group, 0)

    branch(1, q_ref, k1, v1)
    branch(4, q4, k4, v4)
    branch(16, q16, k16, v16)

    gain = g_ref[...]

    def finish(i, c):
        rows = pl.ds(pl.multiple_of(i * ATT_BQ, ATT_BQ), ATT_BQ)
        y = acc[rows, :] / ls[rows, :]
        sq = y * y
        s0 = jnp.sum(jnp.where(head0, sq, 0.0), axis=1, keepdims=True)
        s1 = jnp.sum(jnp.where(head0, 0.0, sq), axis=1, keepdims=True)
        mean = jnp.where(head0, s0, s1) * (1.0 / ATTN_HEAD_DIM)
        o_ref[rows, :] = (y * lax.rsqrt(mean + NORM_EPS) * gain).astype(BF16)
        return c

    lax.fori_loop(0, seq // ATT_BQ, finish, 0)


def _attention(proj, gain, batch, seq):
    t = batch * seq
    padded = seq + 2 * ATTN_RADIUS
    qkv_spec = lambda off: pl.BlockSpec((None, seq, LANES), lambda b, hp: (off + hp, b, 0))
    return pl.pallas_call(
        functools.partial(_attn_body, seq=seq),
        grid=(batch, ATTN_WIDTH // LANES),
        in_specs=[qkv_spec(0), qkv_spec(4), qkv_spec(8),
                  pl.BlockSpec((None, 1, LANES), lambda b, hp: (hp, 0, 0))],
        out_specs=pl.BlockSpec((seq, LANES), lambda b, hp: (b, hp)),
        out_shape=jax.ShapeDtypeStruct((t, ATTN_WIDTH), BF16),
        scratch_shapes=[pltpu.VMEM((seq, LANES), F32),
                        pltpu.VMEM((padded, LANES), BF16), pltpu.VMEM((padded, LANES), BF16),
                        pltpu.VMEM((seq, LANES), BF16),
                        pltpu.VMEM((padded, LANES), BF16), pltpu.VMEM((padded, LANES), BF16),
                        pltpu.VMEM((seq, LANES), BF16),
                        pltpu.VMEM((padded, LANES), BF16), pltpu.VMEM((padded, LANES), BF16),
                        pltpu.VMEM((4, ATT_BQ, ATT_W), F32)]
        + [pltpu.VMEM((seq, LANES), F32)] * 3,
        compiler_params=_cparams(("arbitrary", "arbitrary")),
        name="dilated_attn",
    )(proj, proj, proj, gain)


def _ret_body(lg_ref, q_ref, k_ref, v_ref, gate_ref, gain_ref, o_ref, ob, dec, *, seq):
    c_len = RET_CHUNK
    n_chunks = seq // c_len
    hd = pl.program_id(1)
    lf = lg_ref[0, hd]
    lb = lg_ref[1, hd]
    ri = lax.broadcasted_iota(jnp.int32, (c_len, c_len), 0).astype(F32)
    ci = lax.broadcasted_iota(jnp.int32, (c_len, c_len), 1).astype(F32)
    diff = ri - ci
    dec[0] = jnp.where(diff >= 0, jnp.exp(lf * jnp.maximum(diff, 0.0)),
                       jnp.exp(lb * jnp.maximum(-diff, 0.0)))
    dec[1] = jnp.exp(lf * (ri + 1.0))
    dec[2] = jnp.exp(lf * (c_len - 1.0 - ri))
    dec[3] = jnp.exp(lb * (c_len - ri))
    dec[4] = jnp.exp(lb * ri)
    cdf = jnp.exp(lf * c_len)
    cdb = jnp.exp(lb * c_len)
    tdot = (((0,), (0,)), ((), ()))

    def bwd(jg, state):
        for u in range(RET_UNROLL):
            j = jg * RET_UNROLL + u
            rows = pl.ds(pl.multiple_of((n_chunks - 1 - j) * c_len, c_len), c_len)
            qn = q_ref[rows, :].astype(F32)
            kn = k_ref[rows, :].astype(F32)
            ob[rows, :] = jnp.dot((qn * dec[3]).astype(BF16), state.astype(BF16),
                                  preferred_element_type=F32)
            kv = lax.dot_general((kn * dec[4]).astype(BF16), v_ref[rows, :], tdot,
                                 preferred_element_type=F32)
            state = state * cdb + kv
        return state

    lax.fori_loop(0, n_chunks // RET_UNROLL, bwd, jnp.zeros((c_len, c_len), F32))

    gain = gain_ref[...]

    def fwd(jg, state):
        for u in range(RET_UNROLL):
            rows = pl.ds(pl.multiple_of((jg * RET_UNROLL + u) * c_len, c_len), c_len)
            qb = q_ref[rows, :]
            kb = k_ref[rows, :]
            vb = v_ref[rows, :]
            qn = qb.astype(F32)
            kn = kb.astype(F32)
            sc = lax.dot_general(qb, kb, (((1,), (1,)), ((), ())),
                                 preferred_element_type=F32) * dec[0]
            tot = (ob[rows, :] + jnp.dot(sc.astype(BF16), vb, preferred_element_type=F32)
                   + jnp.dot((qn * dec[1]).astype(BF16), state.astype(BF16),
                             preferred_element_type=F32))
            y = tot - jnp.mean(tot, axis=-1, keepdims=True)
            y = y * lax.rsqrt(jnp.mean(y * y, axis=-1, keepdims=True) + NORM_EPS) * gain
            o_ref[rows, :] = (y * _silu(gate_ref[rows, :].astype(F32))).astype(BF16)
            kv = lax.dot_general((kn * dec[2]).astype(BF16), vb, tdot, preferred_element_type=F32)
            state = state * cdf + kv
        return state

    lax.fori_loop(0, n_chunks // RET_UNROLL, fwd, jnp.zeros((c_len, c_len), F32))


def _retention(proj, gain, lg, batch, seq):
    t = batch * seq
    spec = lambda off: pl.BlockSpec((None, seq, LANES), lambda b, hd: (off + hd, b, 0))
    return pl.pallas_call(
        functools.partial(_ret_body, seq=seq),
        grid=(batch, N_RET_HEADS),
        in_specs=[pl.BlockSpec(memory_space=pltpu.SMEM),
                  spec(12), spec(16), spec(20), spec(24),
                  pl.BlockSpec((None, 1, LANES), lambda b, hd: (hd, 0, 0))],
        out_specs=pl.BlockSpec((seq, LANES), lambda b, hd: (b, hd)),
        out_shape=jax.ShapeDtypeStruct((t, RET_WIDTH), BF16),
        scratch_shapes=[pltpu.VMEM((seq, LANES), F32),
                        pltpu.VMEM((5, RET_CHUNK, RET_CHUNK), F32)],
        compiler_params=_cparams(("arbitrary", "arbitrary")),
        name="retention",
    )(lg, proj, proj, proj, proj, gain)


def _post_dense_body(x_ref, a_ref, r_ref, wo_ref, g_ref, w1_ref, w3_ref, w2_ref, fg_ref, o_ref,
                     *, final_norm):
    x2 = (x_ref[...]
          + jnp.dot(a_ref[...], wo_ref[0:ATTN_WIDTH, :], preferred_element_type=F32)
          + jnp.dot(r_ref[...], wo_ref[ATTN_WIDTH:, :], preferred_element_type=F32))
    h = _rms(x2, g_ref[...]).astype(BF16)
    ffn = None
    for c0, c1 in FF_CHUNKS:
        a = jnp.dot(h, w1_ref[:, c0:c1], preferred_element_type=F32)
        b = jnp.dot(h, w3_ref[:, c0:c1], preferred_element_type=F32)
        u = (_silu(a) * b).astype(BF16)
        d = jnp.dot(u, w2_ref[c0:c1, :], preferred_element_type=F32)
        ffn = d if ffn is None else ffn + d
    y = x2 + ffn
    if final_norm:
        y = _rms(y, fg_ref[...])
    o_ref[...] = y


def _post_dense(x2d, att, ret, wo, g, w1, w3, w2, fg, final_norm):
    t = x2d.shape[0]
    full = lambda a: pl.BlockSpec(a.shape, lambda i: (0,) * a.ndim)
    return pl.pallas_call(
        functools.partial(_post_dense_body, final_norm=final_norm),
        grid=(t // TM,),
        in_specs=[pl.BlockSpec((TM, D_MODEL), lambda i: (i, 0)),
                  pl.BlockSpec((TM, ATTN_WIDTH), lambda i: (i, 0)),
                  pl.BlockSpec((TM, RET_WIDTH), lambda i: (i, 0)),
                  full(wo), full(g), full(w1), full(w3), full(w2), full(fg)],
        out_specs=pl.BlockSpec((TM, D_MODEL), lambda i: (i, 0)),
        out_shape=jax.ShapeDtypeStruct((t, D_MODEL), F32),
        compiler_params=_cparams(("arbitrary",)),
        name="post_dense",
    )(x2d, att, ret, wo, g, w1, w3, w2, fg)


def _post_route_body(x_ref, a_ref, r_ref, wo_ref, g_ref, rt_ref,
                     x2_ref, h_ref, ri_ref, rf_ref, cnt_ref, base):
    @pl.when(pl.program_id(0) == 0)
    def _():
        base[...] = jnp.zeros_like(base)

    x2 = (x_ref[...]
          + jnp.dot(a_ref[...], wo_ref[0:ATTN_WIDTH, :], preferred_element_type=F32)
          + jnp.dot(r_ref[...], wo_ref[ATTN_WIDTH:, :], preferred_element_type=F32))
    x2_ref[...] = x2
    h = _rms(x2, g_ref[...])
    h_ref[...] = h
    logits = jnp.dot(h, rt_ref[...], preferred_element_type=F32, precision=lax.Precision.HIGHEST)
    lane = lax.broadcasted_iota(jnp.int32, (TM, LANES), 1).astype(F32)
    lg = jnp.where(lane < N_EXPERTS, logits, -jnp.inf)
    m1 = jnp.max(lg, axis=1, keepdims=True)
    i1 = jnp.min(jnp.where(lg == m1, lane, float(LANES)), axis=1, keepdims=True)
    lg2 = jnp.where(lane == i1, -jnp.inf, lg)
    m2 = jnp.max(lg2, axis=1, keepdims=True)
    i2 = jnp.min(jnp.where(lg2 == m2, lane, float(LANES)), axis=1, keepdims=True)
    e = jnp.exp(m2 - m1)
    g1 = 1.0 / (1.0 + e)
    g2 = e / (1.0 + e)
    sel1 = lane == i1
    sel2 = lane == i2
    onehot = jnp.where(sel1 | sel2, 1.0, 0.0)
    row = lax.broadcasted_iota(jnp.int32, (TM, TM), 0)
    col = lax.broadcasted_iota(jnp.int32, (TM, TM), 1)
    earlier = jnp.where(col < row, 1.0, 0.0).astype(BF16)
    cnt = jnp.dot(earlier, onehot.astype(BF16), preferred_element_type=F32) + base[...]
    rank1 = jnp.sum(jnp.where(sel1, cnt, 0.0), axis=1, keepdims=True)
    rank2 = jnp.sum(jnp.where(sel2, cnt, 0.0), axis=1, keepdims=True)
    new_base = base[...] + jnp.sum(onehot, axis=0, keepdims=True)
    base[...] = new_base
    cnt_ref[...] = jnp.broadcast_to(new_base, cnt_ref.shape)
    ri = jnp.where(lane == 0, i1, jnp.where(lane == 1, i2,
                   jnp.where(lane == 2, rank1, jnp.where(lane == 3, rank2, 0.0))))
    ri_ref[...] = ri.astype(jnp.int32)
    rf_ref[...] = jnp.where(lane == 0, g1, jnp.where(lane == 1, g2, 0.0))


def _post_route(x2d, att, ret, wo, g, router_pad):
    t = x2d.shape[0]
    full = lambda a: pl.BlockSpec(a.shape, lambda i: (0,) * a.ndim)
    row = lambda w: pl.BlockSpec((TM, w), lambda i: (i, 0))
    return pl.pallas_call(
        _post_route_body,
        grid=(t // TM,),
        in_specs=[row(D_MODEL), row(ATTN_WIDTH), row(RET_WIDTH),
                  full(wo), full(g), full(router_pad)],
        out_specs=[row(D_MODEL), row(D_MODEL), row(LANES), row(LANES),
                   pl.BlockSpec((8, LANES), lambda i: (0, 0))],
        out_shape=[jax.ShapeDtypeStruct((t, D_MODEL), F32),
                   jax.ShapeDtypeStruct((t, D_MODEL), F32),
                   jax.ShapeDtypeStruct((t, LANES), jnp.int32),
                   jax.ShapeDtypeStruct((t, LANES), F32),
                   jax.ShapeDtypeStruct((8, LANES), F32)],
        scratch_shapes=[pltpu.VMEM((1, LANES), F32)],
        compiler_params=_cparams(("arbitrary",)),
        name="post_route",
    )(x2d, att, ret, wo, g, router_pad)


def _row_copy_wait(src, dst, sem):
    pltpu.make_async_copy(src, dst, sem).wait()


def _scatter_body(ginfo, pos_hbm, h_ref, o_hbm, idx, zbuf, sem_i, sem_r, sem_z, *, n_sorted):
    i = pl.program_id(0)

    @pl.when(i == 0)
    def _():
        zbuf[...] = jnp.zeros_like(zbuf)
        total = ginfo[N_EXPERTS - 1]
        fills = []
        for e in range(N_EXPERTS):
            start = pl.multiple_of(jnp.maximum(ginfo[e] - TME, 0), TME)
            fills.append((ginfo[N_EXPERTS + e] > 0, start))
        for j in range(N_EXPERTS):
            start = n_sorted - (j + 1) * TME
            fills.append((start >= total, start))
        for phase in ("start", "wait"):
            for cond, start in fills:
                @pl.when(cond)
                def _(start=start, phase=phase):
                    cp = pltpu.make_async_copy(zbuf, o_hbm.at[pl.ds(start, TME), :], sem_z)
                    cp.start() if phase == "start" else cp.wait()

    cp = pltpu.make_async_copy(pos_hbm.at[i], idx, sem_i)
    cp.start()
    cp.wait()

    def issue(t, c):
        src = h_ref.at[pl.ds(t, 1), :]
        pltpu.make_async_copy(src, o_hbm.at[pl.ds(idx[t], 1), :], sem_r).start()
        pltpu.make_async_copy(src, o_hbm.at[pl.ds(idx[TM + t], 1), :], sem_r).start()
        return c

    lax.fori_loop(0, TM, issue, 0, unroll=DMA_UNROLL)

    def drain(t, c):
        src = h_ref.at[pl.ds(t, 1), :]
        _row_copy_wait(src, o_hbm.at[pl.ds(idx[t], 1), :], sem_r)
        _row_copy_wait(src, o_hbm.at[pl.ds(idx[TM + t], 1), :], sem_r)
        return c

    lax.fori_loop(0, TM, drain, 0, unroll=DMA_UNROLL)


def _scatter(ginfo, pos, h, n_sorted):
    t = h.shape[0]
    return pl.pallas_call(
        functools.partial(_scatter_body, n_sorted=n_sorted),
        grid=(t // TM,),
        in_specs=[pl.BlockSpec(memory_space=pltpu.SMEM),
                  pl.BlockSpec(memory_space=pl.ANY),
                  pl.BlockSpec((TM, D_MODEL), lambda i: (i, 0))],
        out_specs=pl.BlockSpec(memory_space=pl.ANY),
        out_shape=jax.ShapeDtypeStruct((n_sorted, D_MODEL), F32),
        scratch_shapes=[pltpu.SMEM((2 * TM,), jnp.int32),
                        pltpu.VMEM((TME, D_MODEL), F32),
                        pltpu.SemaphoreType.DMA, pltpu.SemaphoreType.DMA,
                        pltpu.SemaphoreType.DMA],
        compiler_params=_cparams(("arbitrary",)),
        name="moe_dispatch",
    )(ginfo, pos, h)


def _expert_body(te_ref, nv_ref, h_ref, w1_ref, w3_ref, w2_ref, o_ref, hb):
    j = pl.program_id(0)
    k = pl.program_id(1)
    nv = nv_ref[j]

    @pl.when((nv == 0) & (k == 0))
    def _():
        o_ref[...] = jnp.zeros_like(o_ref)

    @pl.when(nv > 0)
    def _():
        @pl.when(k == 0)
        def _():
            rows = lax.broadcasted_iota(jnp.int32, (TME, D_MODEL), 0)
            hb[...] = jnp.where(rows < nv, h_ref[...], 0.0).astype(BF16)
            o_ref[...] = jnp.zeros_like(o_ref)

        h = hb[...]
        a = jnp.dot(h, w1_ref[...], preferred_element_type=F32)
        b = jnp.dot(h, w3_ref[...], preferred_element_type=F32)
        u = (_silu(a) * b).astype(BF16)
        o_ref[...] += jnp.dot(u, w2_ref[...], preferred_element_type=F32)


def _experts(tile_e, tile_nv, hs, w1, w3, w2):
    n_sorted = hs.shape[0]
    grid_spec = pltpu.PrefetchScalarGridSpec(
        num_scalar_prefetch=2,
        grid=(n_sorted // TME, D_FF // TK),
        in_specs=[pl.BlockSpec((TME, D_MODEL), lambda j, k, te, nv: (j, 0)),
                  pl.BlockSpec((None, D_MODEL, TK), lambda j, k, te, nv: (te[j], 0, k)),
                  pl.BlockSpec((None, D_MODEL, TK), lambda j, k, te, nv: (te[j], 0, k)),
                  pl.BlockSpec((None, TK, D_MODEL), lambda j, k, te, nv: (te[j], k, 0))],
        out_specs=pl.BlockSpec((TME, D_MODEL), lambda j, k, te, nv: (j, 0)),
        scratch_shapes=[pltpu.VMEM((TME, D_MODEL), BF16)],
    )
    return pl.pallas_call(
        _expert_body,
        grid_spec=grid_spec,
        out_shape=jax.ShapeDtypeStruct((n_sorted, D_MODEL), F32),
        compiler_params=_cparams(("arbitrary", "arbitrary")),
        name="moe_experts",
    )(tile_e, tile_nv, hs, w1, w3, w2)


def _combine_body(pos_hbm, x_ref, rf_ref, fg_ref, y_hbm, o_ref, idx, b1, b2, sem_i, sem_r,
                  *, final_norm):
    i = pl.program_id(0)
    cp = pltpu.make_async_copy(pos_hbm.at[i], idx, sem_i)
    cp.start()
    cp.wait()

    def issue(t, c):
        pltpu.make_async_copy(y_hbm.at[pl.ds(idx[t], 1), :], b1.at[pl.ds(t, 1), :], sem_r).start()
        pltpu.make_async_copy(y_hbm.at[pl.ds(idx[TM + t], 1), :], b2.at[pl.ds(t, 1), :], sem_r).start()
        return c

    lax.fori_loop(0, TM, issue, 0, unroll=DMA_UNROLL)

    def drain(t, c):
        _row_copy_wait(y_hbm.at[pl.ds(idx[t], 1), :], b1.at[pl.ds(t, 1), :], sem_r)
        _row_copy_wait(y_hbm.at[pl.ds(idx[TM + t], 1), :], b2.at[pl.ds(t, 1), :], sem_r)
        return c

    lax.fori_loop(0, TM, drain, 0, unroll=DMA_UNROLL)

    g1 = rf_ref[:, 0:1]
    g2 = rf_ref[:, 1:2]
    y = x_ref[...] + (g1 * b1[...] + g2 * b2[...])
    if final_norm:
        y = _rms(y, fg_ref[...])
    o_ref[...] = y


def _combine(pos, x2, rf, fg, ys, final_norm):
    t = x2.shape[0]
    return pl.pallas_call(
        functools.partial(_combine_body, final_norm=final_norm),
        grid=(t // TM,),
        in_specs=[pl.BlockSpec(memory_space=pl.ANY),
                  pl.BlockSpec((TM, D_MODEL), lambda i: (i, 0)),
                  pl.BlockSpec((TM, LANES), lambda i: (i, 0)),
                  pl.BlockSpec((1, D_MODEL), lambda i: (0, 0)),
                  pl.BlockSpec(memory_space=pl.ANY)],
        out_specs=pl.BlockSpec((TM, D_MODEL), lambda i: (i, 0)),
        out_shape=jax.ShapeDtypeStruct((t, D_MODEL), F32),
        scratch_shapes=[pltpu.SMEM((2 * TM,), jnp.int32),
                        pltpu.VMEM((TM, D_MODEL), F32), pltpu.VMEM((TM, D_MODEL), F32),
                        pltpu.SemaphoreType.DMA, pltpu.SemaphoreType.DMA],
        compiler_params=_cparams(("arbitrary",)),
        name="moe_combine",
    )(pos, x2, rf, fg, ys)


def _moe(x2d, att, ret, wo, g, router_pad, w1, w3, w2, fg, final_norm):
    t = x2d.shape[0]
    n_sorted = 2 * t + N_EXPERTS * TME
    n_tiles = n_sorted // TME
    x2, h, ri, rf, cnt = _post_route(x2d, att, ret, wo, g, router_pad)

    counts = cnt[0, :N_EXPERTS].astype(jnp.int32)
    padded = ((counts + TME - 1) // TME) * TME
    ends = jnp.cumsum(padded)
    starts = ends - padded
    experts = jnp.arange(N_EXPERTS, dtype=jnp.int32)
    start_of = lambda e: jnp.sum(jnp.where(e[:, None] == experts[None, :], starts[None, :], 0), axis=1)
    pos1 = start_of(ri[:, 0]) + ri[:, 2]
    pos2 = start_of(ri[:, 1]) + ri[:, 3]
    pos = jnp.concatenate([pos1.reshape(t // TM, TM), pos2.reshape(t // TM, TM)], axis=1)
    tile_start = jnp.arange(n_tiles, dtype=jnp.int32) * TME
    tile_e = jnp.minimum(jnp.sum(tile_start[:, None] >= ends[None, :], axis=1), N_EXPERTS - 1)
    tile_e = tile_e.astype(jnp.int32)
    tile_nv = jnp.clip(starts[tile_e] + counts[tile_e] - tile_start, 0, TME)
    tile_nv = jnp.where(tile_start < ends[-1], tile_nv, 0).astype(jnp.int32)
    ginfo = jnp.concatenate([ends, padded]).astype(jnp.int32)

    hs = _scatter(ginfo, pos, h, n_sorted)
    ys = _experts(tile_e, tile_nv, hs, w1, w3, w2)
    return _combine(pos, x2, rf, fg, ys, final_norm)


def _rope_tables(seq):
    pos = jnp.arange(seq, dtype=F32)[:, None]
    lane = jnp.arange(LANES)
    j = lane % ATTN_HEAD_DIM
    half = ROPE_DIM // 2
    inv_a = ROPE_THETA ** (-jnp.arange(half, dtype=F32) * 2.0 / ROPE_DIM)
    ang = pos * inv_a[None, :]
    cos_a, sin_a = jnp.cos(ang), jnp.sin(ang)
    jc = jnp.where(j < half, j, j - half) % half
    ca = jnp.where(j[None, :] < ROPE_DIM, cos_a[:, jc], 1.0)
    sa1 = jnp.where(((j >= half) & (j < ROPE_DIM))[None, :], sin_a[:, jc], 0.0)
    sa2 = jnp.where((j < half)[None, :], -sin_a[:, jc], 0.0)
    half_r = RET_HEAD_DIM // 2
    inv_r = RET_THETA ** (-jnp.arange(half_r, dtype=F32) * 2.0 / RET_HEAD_DIM)
    ang_r = pos * inv_r[None, :]
    cos_r, sin_r = jnp.cos(ang_r), jnp.sin(ang_r)
    cr = cos_r[:, lane % half_r]
    sr = jnp.where((lane < half_r)[None, :], -sin_r[:, lane % half_r], sin_r[:, lane % half_r])
    return ca, sa1, sa2, cr, sr


def _trunk(x, w):
    batch, seq, _ = x.shape
    t = batch * seq
    assert seq % (16 * ATT_BQ) == 0 and seq % TM == 0 and t % TM == 0
    tabs = _rope_tables(seq)
    x2d = x.reshape(t, D_MODEL)
    depth = w["w_in"].shape[0]
    for layer in range(depth):
        last = layer == depth - 1
        proj = _in_proj(x2d, w["attn_norm_g"][layer][None, :], w["w_in"][layer], tabs, seq)
        att = _attention(proj, w["attn_out_g"][layer].reshape(ATTN_WIDTH // LANES, 1, LANES), batch, seq)
        ret = _retention(proj, w["ret_out_g"][layer].reshape(N_RET_HEADS, 1, LANES),
                         w["lg"][layer], batch, seq)
        i = layer // 2
        if layer % 2 == 0:
            x2d = _post_dense(x2d, att, ret, w["w_out"][layer], w["ffn_norm_g"][layer][None, :],
                              w["dense_w1"][i], w["dense_w3"][i], w["dense_w2"][i],
                              w["final_norm_g"], last)
        else:
            x2d = _moe(x2d, att, ret, w["w_out"][layer], w["ffn_norm_g"][layer][None, :],
                       w["router_pad"][i], w["moe_w1"][i], w["moe_w3"][i], w["moe_w2"][i],
                       w["final_norm_g"], last)
    return x2d.reshape(batch, seq, D_MODEL)


def kernel(x_prompt, x_sample, attn_norm_g, w_in, attn_out_g, ret_out_g, ret_decay_fwd, ret_decay_bwd, w_out, ffn_norm_g, dense_w1, dense_w3, dense_w2, moe_router, moe_w1, moe_w3, moe_w2, final_norm_g):
    lg_f = jnp.log1p(-jnp.exp(ret_decay_fwd.astype(F32)))
    lg_b = jnp.log1p(-jnp.exp(ret_decay_bwd.astype(F32)))
    w = {
        "attn_norm_g": attn_norm_g, "attn_out_g": attn_out_g, "ret_out_g": ret_out_g,
        "ffn_norm_g": ffn_norm_g, "final_norm_g": final_norm_g[None, :],
        "lg": jnp.stack([lg_f, lg_b], axis=1),
        "w_in": w_in.astype(BF16), "w_out": w_out.astype(BF16),
        "dense_w1": dense_w1.astype(BF16), "dense_w3": dense_w3.astype(BF16),
        "dense_w2": dense_w2.astype(BF16),
        "moe_w1": moe_w1.astype(BF16), "moe_w3": moe_w3.astype(BF16), "moe_w2": moe_w2.astype(BF16),
        "router_pad": jnp.pad(moe_router.astype(F32), ((0, 0), (0, 0), (0, LANES - N_EXPERTS))),
    }
    return _trunk(x_prompt, w), _trunk(x_sample, w)
```

```python
import functools
import math

import jax
import jax.numpy as jnp
from jax import lax
from jax.experimental import pallas as pl
from jax.experimental.pallas import tpu as pltpu

F32 = jnp.float32
BF16 = jnp.bfloat16

D_MODEL = 1024
N_ATTN_HEADS = 8
ATTN_HEAD_DIM = 64
ATTN_WIDTH = N_ATTN_HEADS * ATTN_HEAD_DIM
ATTN_RADIUS = 64
DILATIONS = (1, 4, 16)
ROPE_THETA = 500000.0
ROPE_DIM = ATTN_HEAD_DIM // 4
N_RET_HEADS = 4
RET_HEAD_DIM = 128
RET_WIDTH = N_RET_HEADS * RET_HEAD_DIM
RET_THETA = 10000.0
RET_CHUNK = 128
IN_COLS = 3 * ATTN_WIDTH + 4 * RET_WIDTH
N_EXPERTS = 8
D_FF = 2816
NORM_EPS = 1e-6
NEG_INF = -1e30
TINY = 1e-30

LANES = 128
N_COL_BLOCKS = IN_COLS // LANES
VMEM_LIMIT = 56 * 1024 * 1024

TM = 512
TME = 512
FF_CHUNKS = ((0, 1024), (1024, 2048), (2048, 2816))
ATT_BQ = 128
ATT_W = ATT_BQ + 2 * ATTN_RADIUS
ATT_NB = 8
RET_UNROLL = 8
DMA_UNROLL = 8


def _cparams(sem):
    return pltpu.CompilerParams(dimension_semantics=sem, vmem_limit_bytes=VMEM_LIMIT)


def _rms(x, g):
    return x * lax.rsqrt(jnp.mean(x * x, axis=-1, keepdims=True) + NORM_EPS) * g


def _silu(x):
    return x / (1.0 + jnp.exp(-x))


def _in_proj_body(x_ref, g_ref, w_ref, ca_ref, sa1_ref, sa2_ref, cr_ref, sr_ref, o_ref):
    h = _rms(x_ref[...], g_ref[...]).astype(BF16)
    for c in range(7):
        acc = jnp.dot(h, w_ref[:, c * 512:(c + 1) * 512], preferred_element_type=F32)
        for s in range(4):
            a = acc[:, s * LANES:(s + 1) * LANES]
            if c in (0, 1):
                a = (a * ca_ref[...] + pltpu.roll(a, ROPE_DIM // 2, 1) * sa1_ref[...]
                     + pltpu.roll(a, LANES - ROPE_DIM // 2, 1) * sa2_ref[...])
                if c == 0:
                    a = a * (ATTN_HEAD_DIM ** -0.5)
            elif c in (3, 4):
                a = a * cr_ref[...] + pltpu.roll(a, RET_HEAD_DIM // 2, 1) * sr_ref[...]
                if c == 4:
                    a = a * (RET_HEAD_DIM ** -0.5)
            o_ref[c * 4 + s] = a.astype(BF16)


def _in_proj(x2d, g, w, tabs, seq):
    t = x2d.shape[0]
    nblk = seq // TM
    tab_spec = pl.BlockSpec((TM, LANES), lambda i: (i % nblk, 0))
    return pl.pallas_call(
        _in_proj_body,
        grid=(t // TM,),
        in_specs=[pl.BlockSpec((TM, D_MODEL), lambda i: (i, 0)),
                  pl.BlockSpec((1, D_MODEL), lambda i: (0, 0)),
                  pl.BlockSpec((D_MODEL, IN_COLS), lambda i: (0, 0)),
                  tab_spec, tab_spec, tab_spec, tab_spec, tab_spec],
        out_specs=pl.BlockSpec((N_COL_BLOCKS, TM, LANES), lambda i: (0, i, 0)),
        out_shape=jax.ShapeDtypeStruct((N_COL_BLOCKS, t, LANES), BF16),
        compiler_params=_cparams(("arbitrary",)),
        name="in_proj",
    )(x2d, g, w, *tabs)


def _attn_body(q_ref, k_ref, v_ref, g_ref, o_ref,
               xf, k1, v1, q4, k4, v4, q16, k16, v16, bias, acc, ms, ls, *, seq):
    stage = 512
    pad = ATTN_RADIUS
    lane = lax.broadcasted_iota(jnp.int32, (ATT_BQ, LANES), 1)
    head0 = lane < ATTN_HEAD_DIM
    khead0 = lax.broadcasted_iota(jnp.int32, (ATT_W, LANES), 1) < ATTN_HEAD_DIM

    qi = lax.broadcasted_iota(jnp.int32, (ATT_BQ, ATT_W), 0)
    kj = lax.broadcasted_iota(jnp.int32, (ATT_BQ, ATT_W), 1)
    band = jnp.abs(kj - pad - qi) <= ATTN_RADIUS
    for e in range(4):
        ok = band
        if e & 1:
            ok = ok & (kj >= pad)
        if e & 2:
            ok = ok & (kj < ATT_BQ + pad)
        bias[e] = jnp.where(ok, 0.0, NEG_INF)

    for arr in (k1, v1, k4, v4, k16, v16):
        arr[pl.ds(0, pad), :] = jnp.zeros((pad, LANES), BF16)
        arr[pl.ds(seq + pad, pad), :] = jnp.zeros((pad, LANES), BF16)

    for src, d1, d4, d16 in ((q_ref, None, q4, q16), (k_ref, k1, k4, k16), (v_ref, v1, v4, v16)):
        off = 0 if d1 is None else pad

        def upcast(i, c, src=src, d1=d1):
            r0 = pl.multiple_of(i * stage, stage)
            x = src[pl.ds(r0, stage), :]
            xf[pl.ds(r0, stage), :] = x.astype(F32)
            if d1 is not None:
                d1[pl.ds(pad + r0, stage), :] = x
            return c
        lax.fori_loop(0, seq // stage, upcast, 0)
        for d, dst in ((4, d4), (16, d16)):
            cls = seq // d
            ch = min(cls, 256)
            for r in range(d):
                def deinterleave(i, c, d=d, dst=dst, r=r, cls=cls, ch=ch, off=off):
                    i0 = pl.multiple_of(i * ch, ch)
                    dst[pl.ds(off + r * cls + i0, ch), :] = (
                        xf[pl.ds(r + d * i0, ch, stride=d), :].astype(BF16))
                    return c
                lax.fori_loop(0, cls // ch, deinterleave, 0)

    def branch(d, qd, kd, vd):
        cls = seq // d
        nblk = cls // ATT_BQ
        shift = nblk.bit_length() - 1

        def group(g, c):
            blocks = []
            for u in range(ATT_NB):
                t = g * ATT_NB + u
                base = pl.multiple_of(t * ATT_BQ, ATT_BQ)
                n = t & (nblk - 1)
                r = t >> shift
                edge = jnp.where(n == 0, 1, 0) + jnp.where(n == nblk - 1, 2, 0)
                blocks.append((base, n, r, edge))
            scores = []
            for base, n, r, edge in blocks:
                qb = qd[pl.ds(base, ATT_BQ), :]
                kw = kd[pl.ds(base, ATT_W), :]
                for hd in range(2):
                    qh = jnp.where(head0 if hd == 0 else ~head0, qb, jnp.zeros_like(qb))
                    scores.append(lax.dot_general(qh, kw, (((1,), (1,)), ((), ())),
                                                  preferred_element_type=F32) + bias[edge])
            maxes = [jnp.max(s, axis=1, keepdims=True) for s in scores]
            probs = [jnp.exp(s - m).astype(BF16) for s, m in zip(scores, maxes)]
            fulls = []
            for i, (base, n, r, edge) in enumerate(blocks):
                vw = vd[pl.ds(base, ATT_W), :]
                for hd in range(2):
                    vh = jnp.where(khead0 if hd == 0 else ~khead0, vw, jnp.ones_like(vw))
                    fulls.append(jnp.dot(probs[2 * i + hd], vh, preferred_element_type=F32))
            for i, (base, n, r, edge) in enumerate(blocks):
                f0, f1 = fulls[2 * i], fulls[2 * i + 1]
                o_new = jnp.where(head0, f0, f1)
                l_new = pltpu.roll(jnp.where(head0, f1, f0), ATTN_HEAD_DIM, 1)
                l_new = jnp.maximum(l_new, TINY)
                m_new = jnp.where(head0, maxes[2 * i], maxes[2 * i + 1])
                if d == 1:
                    rows = pl.ds(base, ATT_BQ)
                    acc[rows, :] = o_new
                    ms[rows, :] = m_new
                    ls[rows, :] = l_new
                else:
                    rows = pl.ds(d * (n * ATT_BQ) + r, ATT_BQ, stride=d)
                    m_old = ms[rows, :]
                    m_tot = jnp.maximum(m_old, m_new)
                    a_old = jnp.exp(m_old - m_tot)
                    a_new = jnp.exp(m_new - m_tot)
                    acc[rows, :] = acc[rows, :] * a_old + o_new * a_new
                    ls[rows, :] = ls[rows, :] * a_old + l_new * a_new
                    ms[rows, :] = m_tot
            return c

        lax.fori_loop(0, seq // (ATT_BQ * ATT_NB), group, 0)

    branch(1, q_ref, k1, v1)
    branch(4, q4, k4, v4)
    branch(16, q16, k16, v16)

    gain = g_ref[...]

    def finish(i, c):
        rows = pl.ds(pl.multiple_of(i * ATT_BQ, ATT_BQ), ATT_BQ)
        y = acc[rows, :] / ls[rows, :]
        sq = y * y
        s0 = jnp.sum(jnp.where(head0, sq, 0.0), axis=1, keepdims=True)
        s1 = jnp.sum(jnp.where(head0, 0.0, sq), axis=1, keepdims=True)
        mean = jnp.where(head0, s0, s1) * (1.0 / ATTN_HEAD_DIM)
        o_ref[rows, :] = (y * lax.rsqrt(mean + NORM_EPS) * gain).astype(BF16)
        return c

    lax.fori_loop(0, seq // ATT_BQ, finish, 0)


def _attention(proj, gain, batch, seq):
    t = batch * seq
    padded = seq + 2 * ATTN_RADIUS
    qkv_spec = lambda off: pl.BlockSpec((None, seq, LANES), lambda b, hp: (off + hp, b, 0))
    return pl.pallas_call(
        functools.partial(_attn_body, seq=seq),
        grid=(batch, ATTN_WIDTH // LANES),
        in_specs=[qkv_spec(0), qkv_spec(4), qkv_spec(8),
                  pl.BlockSpec((None, 1, LANES), lambda b, hp: (hp, 0, 0))],
        out_specs=pl.BlockSpec((seq, LANES), lambda b, hp: (b, hp)),
        out_shape=jax.ShapeDtypeStruct((t, ATTN_WIDTH), BF16),
        scratch_shapes=[pltpu.VMEM((seq, LANES), F32),
                        pltpu.VMEM((padded, LANES), BF16), pltpu.VMEM((padded, LANES), BF16),
                        pltpu.VMEM((seq, LANES), BF16),
                        pltpu.VMEM((padded, LANES), BF16), pltpu.VMEM((padded, LANES), BF16),
                        pltpu.VMEM((seq, LANES), BF16),
                        pltpu.VMEM((padded, LANES), BF16), pltpu.VMEM((padded, LANES), BF16),
                        pltpu.VMEM((4, ATT_BQ, ATT_W), F32)]
        + [pltpu.VMEM((seq, LANES), F32)] * 3,
        compiler_params=_cparams(("arbitrary", "arbitrary")),
        name="dilated_attn",
    )(proj, proj, proj, gain)


def _ret_body(lg_ref, q_ref, k_ref, v_ref, gate_ref, gain_ref, o_ref, ob, dec, *, seq):
    c_len = RET_CHUNK
    n_chunks = seq // c_len
    hd = pl.program_id(1)
    lf = lg_ref[0, hd]
    lb = lg_ref[1, hd]
    ri = lax.broadcasted_iota(jnp.int32, (c_len, c_len), 0).astype(F32)
    ci = lax.broadcasted_iota(jnp.int32, (c_len, c_len), 1).astype(F32)
    diff = ri - ci
    dec[0] = jnp.where(diff >= 0, jnp.exp(lf * jnp.maximum(diff, 0.0)),
                       jnp.exp(lb * jnp.maximum(-diff, 0.0)))
    dec[1] = jnp.exp(lf * (ri + 1.0))
    dec[2] = jnp.exp(lf * (c_len - 1.0 - ri))
    dec[3] = jnp.exp(lb * (c_len - ri))
    dec[4] = jnp.exp(lb * ri)
    cdf = jnp.exp(lf * c_len)
    cdb = jnp.exp(lb * c_len)
    tdot = (((0,), (0,)), ((), ()))

    def bwd(jg, state):
        for u in range(RET_UNROLL):
            j = jg * RET_UNROLL + u
            rows = pl.ds(pl.multiple_of((n_chunks - 1 - j) * c_len, c_len), c_len)
            qn = q_ref[rows, :].astype(F32)
            kn = k_ref[rows, :].astype(F32)
            ob[rows, :] = jnp.dot((qn * dec[3]).astype(BF16), state.astype(BF16),
                                  preferred_element_type=F32)
            kv = lax.dot_general((kn * dec[4]).astype(BF16), v_ref[rows, :], tdot,
                                 preferred_element_type=F32)
            state = state * cdb + kv
        return state

    lax.fori_loop(0, n_chunks // RET_UNROLL, bwd, jnp.zeros((c_len, c_len), F32))

    gain = gain_ref[...]

    def fwd(jg, state):
        row_sets = [pl.ds(pl.multiple_of((jg * RET_UNROLL + u) * c_len, c_len), c_len)
                    for u in range(RET_UNROLL)]
        states = []
        for rows in row_sets:
            states.append(state)
            kn = k_ref[rows, :].astype(F32)
            kv = lax.dot_general((kn * dec[2]).astype(BF16), v_ref[rows, :], tdot,
                                 preferred_element_type=F32)
            state = state * cdf + kv
        for rows, st in zip(row_sets, states):
            qb = q_ref[rows, :]
            vb = v_ref[rows, :]
            sc = lax.dot_general(qb, k_ref[rows, :], (((1,), (1,)), ((), ())),
                                 preferred_element_type=F32) * dec[0]
            tot = (ob[rows, :] + jnp.dot(sc.astype(BF16), vb, preferred_element_type=F32)
                   + jnp.dot((qb.astype(F32) * dec[1]).astype(BF16), st.astype(BF16),
                             preferred_element_type=F32))
            y = tot - jnp.mean(tot, axis=-1, keepdims=True)
            y = y * lax.rsqrt(jnp.mean(y * y, axis=-1, keepdims=True) + NORM_EPS) * gain
            o_ref[rows, :] = (y * _silu(gate_ref[rows, :].astype(F32))).astype(BF16)
        return state

    lax.fori_loop(0, n_chunks // RET_UNROLL, fwd, jnp.zeros((c_len, c_len), F32))


def _retention(proj, gain, lg, batch, seq):
    t = batch * seq
    spec = lambda off: pl.BlockSpec((None, seq, LANES), lambda b, hd: (off + hd, b, 0))
    return pl.pallas_call(
        functools.partial(_ret_body, seq=seq),
        grid=(batch, N_RET_HEADS),
        in_specs=[pl.BlockSpec(memory_space=pltpu.SMEM),
                  spec(12), spec(16), spec(20), spec(24),
                  pl.BlockSpec((None, 1, LANES), lambda b, hd: (hd, 0, 0))],
        out_specs=pl.BlockSpec((seq, LANES), lambda b, hd: (b, hd)),
        out_shape=jax.ShapeDtypeStruct((t, RET_WIDTH), BF16),
        scratch_shapes=[pltpu.VMEM((seq, LANES), F32),
                        pltpu.VMEM((5, RET_CHUNK, RET_CHUNK), F32)],
        compiler_params=_cparams(("arbitrary", "arbitrary")),
        name="retention",
    )(lg, proj, proj, proj, proj, gain)


def _post_dense_body(x_ref, a_ref, r_ref, wo_ref, g_ref, w1_ref, w3_ref, w2_ref, fg_ref, o_ref,
                     *, final_norm):
    x2 = (x_ref[...]
          + jnp.dot(a_ref[...], wo_ref[0:ATTN_WIDTH, :], preferred_element_type=F32)
          + jnp.dot(r_ref[...], wo_ref[ATTN_WIDTH:, :], preferred_element_type=F32))
    h = _rms(x2, g_ref[...]).astype(BF16)
    ffn = None
    for c0, c1 in FF_CHUNKS:
        a = jnp.dot(h, w1_ref[:, c0:c1], preferred_element_type=F32)
        b = jnp.dot(h, w3_ref[:, c0:c1], preferred_element_type=F32)
        u = (_silu(a) * b).astype(BF16)
        d = jnp.dot(u, w2_ref[c0:c1, :], preferred_element_type=F32)
        ffn = d if ffn is None else ffn + d
    y = x2 + ffn
    if final_norm:
        y = _rms(y, fg_ref[...])
    o_ref[...] = y


def _post_dense(x2d, att, ret, wo, g, w1, w3, w2, fg, final_norm):
    t = x2d.shape[0]
    full = lambda a: pl.BlockSpec(a.shape, lambda i: (0,) * a.ndim)
    return pl.pallas_call(
        functools.partial(_post_dense_body, final_norm=final_norm),
        grid=(t // TM,),
        in_specs=[pl.BlockSpec((TM, D_MODEL), lambda i: (i, 0)),
                  pl.BlockSpec((TM, ATTN_WIDTH), lambda i: (i, 0)),
                  pl.BlockSpec((TM, RET_WIDTH), lambda i: (i, 0)),
                  full(wo), full(g), full(w1), full(w3), full(w2), full(fg)],
        out_specs=pl.BlockSpec((TM, D_MODEL), lambda i: (i, 0)),
        out_shape=jax.ShapeDtypeStruct((t, D_MODEL), F32),
        compiler_params=_cparams(("arbitrary",)),
        name="post_dense",
    )(x2d, att, ret, wo, g, w1, w3, w2, fg)


def _post_route_body(x_ref, a_ref, r_ref, wo_ref, g_ref, rt_ref,
                     x2_ref, h_ref, ri_ref, rf_ref, cnt_ref, base):
    @pl.when(pl.program_id(0) == 0)
    def _():
        base[...] = jnp.zeros_like(base)

    x2 = (x_ref[...]
          + jnp.dot(a_ref[...], wo_ref[0:ATTN_WIDTH, :], preferred_element_type=F32)
          + jnp.dot(r_ref[...], wo_ref[ATTN_WIDTH:, :], preferred_element_type=F32))
    x2_ref[...] = x2
    h = _rms(x2, g_ref[...])
    h_ref[...] = h
    hi = h.astype(BF16)
    lo = (h - hi.astype(F32)).astype(BF16)
    parts = (jnp.dot(hi, rt_ref[...], preferred_element_type=F32)
             + jnp.dot(lo, rt_ref[...], preferred_element_type=F32))
    logits = parts + pltpu.roll(parts, LANES - N_EXPERTS, 1)
    lane = lax.broadcasted_iota(jnp.int32, (TM, LANES), 1).astype(F32)
    lg = jnp.where(lane < N_EXPERTS, logits, -jnp.inf)
    m1 = jnp.max(lg, axis=1, keepdims=True)
    i1 = jnp.min(jnp.where(lg == m1, lane, float(LANES)), axis=1, keepdims=True)
    lg2 = jnp.where(lane == i1, -jnp.inf, lg)
    m2 = jnp.max(lg2, axis=1, keepdims=True)
    i2 = jnp.min(jnp.where(lg2 == m2, lane, float(LANES)), axis=1, keepdims=True)
    e = jnp.exp(m2 - m1)
    g1 = 1.0 / (1.0 + e)
    g2 = e / (1.0 + e)
    sel1 = lane == i1
    sel2 = lane == i2
    onehot = jnp.where(sel1 | sel2, 1.0, 0.0)
    row = lax.broadcasted_iota(jnp.int32, (TM, TM), 0)
    col = lax.broadcasted_iota(jnp.int32, (TM, TM), 1)
    earlier = jnp.where(col < row, 1.0, 0.0).astype(BF16)
    cnt = jnp.dot(earlier, onehot.astype(BF16), preferred_element_type=F32) + base[...]
    rank1 = jnp.sum(jnp.where(sel1, cnt, 0.0), axis=1, keepdims=True)
    rank2 = jnp.sum(jnp.where(sel2, cnt, 0.0), axis=1, keepdims=True)
    new_base = base[...] + jnp.sum(onehot, axis=0, keepdims=True)
    base[...] = new_base
    cnt_ref[...] = jnp.broadcast_to(new_base, cnt_ref.shape)
    ri = jnp.where(lane == 0, i1, jnp.where(lane == 1, i2,
                   jnp.where(lane == 2, rank1, jnp.where(lane == 3, rank2, 0.0))))
    ri_ref[...] = ri.astype(jnp.int32)
    rf_ref[...] = jnp.where(lane == 0, g1, jnp.where(lane == 1, g2, 0.0))


def _post_route(x2d, att, ret, wo, g, router_pad):
    t = x2d.shape[0]
    full = lambda a: pl.BlockSpec(a.shape, lambda i: (0,) * a.ndim)
    row = lambda w: pl.BlockSpec((TM, w), lambda i: (i, 0))
    return pl.pallas_call(
        _post_route_body,
        grid=(t // TM,),
        in_specs=[row(D_MODEL), row(ATTN_WIDTH), row(RET_WIDTH),
                  full(wo), full(g), full(router_pad)],
        out_specs=[row(D_MODEL), row(D_MODEL), row(LANES), row(LANES),
                   pl.BlockSpec((8, LANES), lambda i: (0, 0))],
        out_shape=[jax.ShapeDtypeStruct((t, D_MODEL), F32),
                   jax.ShapeDtypeStruct((t, D_MODEL), F32),
                   jax.ShapeDtypeStruct((t, LANES), jnp.int32),
                   jax.ShapeDtypeStruct((t, LANES), F32),
                   jax.ShapeDtypeStruct((8, LANES), F32)],
        scratch_shapes=[pltpu.VMEM((1, LANES), F32)],
        compiler_params=_cparams(("arbitrary",)),
        name="post_route",
    )(x2d, att, ret, wo, g, router_pad)


def _row_copy_wait(src, dst, sem):
    pltpu.make_async_copy(src, dst, sem).wait()


def _scatter_body(ginfo, pos_hbm, h_ref, o_hbm, idx, zbuf, sem_i, sem_r, sem_z, *, n_sorted):
    i = pl.program_id(0)

    @pl.when(i == 0)
    def _():
        zbuf[...] = jnp.zeros_like(zbuf)
        total = ginfo[N_EXPERTS - 1]
        fills = []
        for e in range(N_EXPERTS):
            start = pl.multiple_of(jnp.maximum(ginfo[e] - TME, 0), TME)
            fills.append((ginfo[N_EXPERTS + e] > 0, start))
        for j in range(N_EXPERTS):
            start = n_sorted - (j + 1) * TME
            fills.append((start >= total, start))
        for phase in ("start", "wait"):
            for cond, start in fills:
                @pl.when(cond)
                def _(start=start, phase=phase):
                    cp = pltpu.make_async_copy(zbuf, o_hbm.at[pl.ds(start, TME), :], sem_z)
                    cp.start() if phase == "start" else cp.wait()

    cp = pltpu.make_async_copy(pos_hbm.at[i], idx, sem_i)
    cp.start()
    cp.wait()

    def issue(t, c):
        src = h_ref.at[pl.ds(t, 1), :]
        pltpu.make_async_copy(src, o_hbm.at[pl.ds(idx[t], 1), :], sem_r).start()
        pltpu.make_async_copy(src, o_hbm.at[pl.ds(idx[TM + t], 1), :], sem_r).start()
        return c

    lax.fori_loop(0, TM, issue, 0, unroll=DMA_UNROLL)

    def drain(t, c):
        src = h_ref.at[pl.ds(t, 1), :]
        _row_copy_wait(src, o_hbm.at[pl.ds(idx[t], 1), :], sem_r)
        _row_copy_wait(src, o_hbm.at[pl.ds(idx[TM + t], 1), :], sem_r)
        return c

    lax.fori_loop(0, TM, drain, 0, unroll=DMA_UNROLL)


def _scatter(ginfo, pos, h, n_sorted):
    t = h.shape[0]
    return pl.pallas_call(
        functools.partial(_scatter_body, n_sorted=n_sorted),
        grid=(t // TM,),
        in_specs=[pl.BlockSpec(memory_space=pltpu.SMEM),
                  pl.BlockSpec(memory_space=pl.ANY),
                  pl.BlockSpec((TM, D_MODEL), lambda i: (i, 0))],
        out_specs=pl.BlockSpec(memory_space=pl.ANY),
        out_shape=jax.ShapeDtypeStruct((n_sorted, D_MODEL), F32),
        scratch_shapes=[pltpu.SMEM((2 * TM,), jnp.int32),
                        pltpu.VMEM((TME, D_MODEL), F32),
                        pltpu.SemaphoreType.DMA, pltpu.SemaphoreType.DMA,
                        pltpu.SemaphoreType.DMA],
        compiler_params=_cparams(("arbitrary",)),
        name="moe_dispatch",
    )(ginfo, pos, h)


def _expert_body(te_ref, nv_ref, h_ref, w1_ref, w3_ref, w2_ref, o_ref):
    nv = nv_ref[pl.program_id(0)]

    @pl.when(nv == 0)
    def _():
        o_ref[...] = jnp.zeros_like(o_ref)

    @pl.when(nv > 0)
    def _():
        rows = lax.broadcasted_iota(jnp.int32, (TME, D_MODEL), 0)
        h = jnp.where(rows < nv, h_ref[...], 0.0).astype(BF16)
        ffn = None
        for c0, c1 in FF_CHUNKS:
            a = jnp.dot(h, w1_ref[:, c0:c1], preferred_element_type=F32)
            b = jnp.dot(h, w3_ref[:, c0:c1], preferred_element_type=F32)
            u = (_silu(a) * b).astype(BF16)
            d = jnp.dot(u, w2_ref[c0:c1, :], preferred_element_type=F32)
            ffn = d if ffn is None else ffn + d
        o_ref[...] = ffn


def _experts(tile_e, tile_nv, hs, w1, w3, w2):
    n_sorted = hs.shape[0]
    grid_spec = pltpu.PrefetchScalarGridSpec(
        num_scalar_prefetch=2,
        grid=(n_sorted // TME,),
        in_specs=[pl.BlockSpec((TME, D_MODEL), lambda j, te, nv: (j, 0)),
                  pl.BlockSpec((None, D_MODEL, D_FF), lambda j, te, nv: (te[j], 0, 0)),
                  pl.BlockSpec((None, D_MODEL, D_FF), lambda j, te, nv: (te[j], 0, 0)),
                  pl.BlockSpec((None, D_FF, D_MODEL), lambda j, te, nv: (te[j], 0, 0))],
        out_specs=pl.BlockSpec((TME, D_MODEL), lambda j, te, nv: (j, 0)),
    )
    return pl.pallas_call(
        _expert_body,
        grid_spec=grid_spec,
        out_shape=jax.ShapeDtypeStruct((n_sorted, D_MODEL), F32),
        compiler_params=_cparams(("arbitrary",)),
        name="moe_experts",
    )(tile_e, tile_nv, hs, w1, w3, w2)


def _combine_body(pos_hbm, x_ref, rf_ref, fg_ref, y_hbm, o_ref, idx, b1, b2, sem_i, sem_r,
                  *, final_norm):
    i = pl.program_id(0)
    cp = pltpu.make_async_copy(pos_hbm.at[i], idx, sem_i)
    cp.start()
    cp.wait()

    def issue(t, c):
        pltpu.make_async_copy(y_hbm.at[pl.ds(idx[t], 1), :], b1.at[pl.ds(t, 1), :], sem_r).start()
        pltpu.make_async_copy(y_hbm.at[pl.ds(idx[TM + t], 1), :], b2.at[pl.ds(t, 1), :], sem_r).start()
        return c

    lax.fori_loop(0, TM, issue, 0, unroll=DMA_UNROLL)

    def drain(t, c):
        _row_copy_wait(y_hbm.at[pl.ds(idx[t], 1), :], b1.at[pl.ds(t, 1), :], sem_r)
        _row_copy_wait(y_hbm.at[pl.ds(idx[TM + t], 1), :], b2.at[pl.ds(t, 1), :], sem_r)
        return c

    lax.fori_loop(0, TM, drain, 0, unroll=DMA_UNROLL)

    g1 = rf_ref[:, 0:1]
    g2 = rf_ref[:, 1:2]
    y = x_ref[...] + (g1 * b1[...] + g2 * b2[...])
    if final_norm:
        y = _rms(y, fg_ref[...])
    o_ref[...] = y


def _combine(pos, x2, rf, fg, ys, final_norm):
    t = x2.shape[0]
    return pl.pallas_call(
        functools.partial(_combine_body, final_norm=final_norm),
        grid=(t // TM,),
        in_specs=[pl.BlockSpec(memory_space=pl.ANY),
                  pl.BlockSpec((TM, D_MODEL), lambda i: (i, 0)),
                  pl.BlockSpec((TM, LANES), lambda i: (i, 0)),
                  pl.BlockSpec((1, D_MODEL), lambda i: (0, 0)),
                  pl.BlockSpec(memory_space=pl.ANY)],
        out_specs=pl.BlockSpec((TM, D_MODEL), lambda i: (i, 0)),
        out_shape=jax.ShapeDtypeStruct((t, D_MODEL), F32),
        scratch_shapes=[pltpu.SMEM((2 * TM,), jnp.int32),
                        pltpu.VMEM((TM, D_MODEL), F32), pltpu.VMEM((TM, D_MODEL), F32),
                        pltpu.SemaphoreType.DMA, pltpu.SemaphoreType.DMA],
        compiler_params=_cparams(("arbitrary",)),
        name="moe_combine",
    )(pos, x2, rf, fg, ys)


def _moe(x2d, att, ret, wo, g, router_pad, w1, w3, w2, fg, final_norm):
    t = x2d.shape[0]
    n_sorted = 2 * t + N_EXPERTS * TME
    n_tiles = n_sorted // TME
    x2, h, ri, rf, cnt = _post_route(x2d, att, ret, wo, g, router_pad)

    counts = cnt[0, :N_EXPERTS].astype(jnp.int32)
    padded = ((counts + TME - 1) // TME) * TME
    ends = jnp.cumsum(padded)
    starts = ends - padded
    experts = jnp.arange(N_EXPERTS, dtype=jnp.int32)
    start_of = lambda e: jnp.sum(jnp.where(e[:, None] == experts[None, :], starts[None, :], 0), axis=1)
    pos1 = start_of(ri[:, 0]) + ri[:, 2]
    pos2 = start_of(ri[:, 1]) + ri[:, 3]
    pos = jnp.concatenate([pos1.reshape(t // TM, TM), pos2.reshape(t // TM, TM)], axis=1)
    tile_start = jnp.arange(n_tiles, dtype=jnp.int32) * TME
    tile_e = jnp.minimum(jnp.sum(tile_start[:, None] >= ends[None, :], axis=1), N_EXPERTS - 1)
    tile_e = tile_e.astype(jnp.int32)
    tile_nv = jnp.clip(starts[tile_e] + counts[tile_e] - tile_start, 0, TME)
    tile_nv = jnp.where(tile_start < ends[-1], tile_nv, 0).astype(jnp.int32)
    ginfo = jnp.concatenate([ends, padded]).astype(jnp.int32)

    hs = _scatter(ginfo, pos, h, n_sorted)
    ys = _experts(tile_e, tile_nv, hs, w1, w3, w2)
    return _combine(pos, x2, rf, fg, ys, final_norm)


def _rope_tables(seq):
    pos = jnp.arange(seq, dtype=F32)[:, None]
    lane = jnp.arange(LANES)
    j = lane % ATTN_HEAD_DIM
    half = ROPE_DIM // 2
    inv_a = ROPE_THETA ** (-jnp.arange(half, dtype=F32) * 2.0 / ROPE_DIM)
    ang = pos * inv_a[None, :]
    cos_a, sin_a = jnp.cos(ang), jnp.sin(ang)
    jc = jnp.where(j < half, j, j - half) % half
    ca = jnp.where(j[None, :] < ROPE_DIM, cos_a[:, jc], 1.0)
    sa1 = jnp.where(((j >= half) & (j < ROPE_DIM))[None, :], sin_a[:, jc], 0.0)
    sa2 = jnp.where((j < half)[None, :], -sin_a[:, jc], 0.0)
    half_r = RET_HEAD_DIM // 2
    inv_r = RET_THETA ** (-jnp.arange(half_r, dtype=F32) * 2.0 / RET_HEAD_DIM)
    ang_r = pos * inv_r[None, :]
    cos_r, sin_r = jnp.cos(ang_r), jnp.sin(ang_r)
    cr = cos_r[:, lane % half_r]
    sr = jnp.where((lane < half_r)[None, :], -sin_r[:, lane % half_r], sin_r[:, lane % half_r])
    return ca, sa1, sa2, cr, sr


def _pack_router(router):
    hi = router.astype(BF16)
    lo = (router - hi.astype(F32)).astype(BF16)
    packed = jnp.concatenate([hi, lo], axis=-1)
    return jnp.pad(packed, ((0, 0), (0, 0), (0, LANES - 2 * N_EXPERTS)))


def _trunk(x, w):
    batch, seq, _ = x.shape
    t = batch * seq
    assert seq % (16 * ATT_BQ) == 0 and seq % TM == 0 and t % TM == 0
    tabs = _rope_tables(seq)
    x2d = x.reshape(t, D_MODEL)
    depth = w["w_in"].shape[0]
    for layer in range(depth):
        last = layer == depth - 1
        proj = _in_proj(x2d, w["attn_norm_g"][layer][None, :], w["w_in"][layer], tabs, seq)
        att = _attention(proj, w["attn_out_g"][layer].reshape(ATTN_WIDTH // LANES, 1, LANES), batch, seq)
        ret = _retention(proj, w["ret_out_g"][layer].reshape(N_RET_HEADS, 1, LANES),
                         w["lg"][layer], batch, seq)
        i = layer // 2
        if layer % 2 == 0:
            x2d = _post_dense(x2d, att, ret, w["w_out"][layer], w["ffn_norm_g"][layer][None, :],
                              w["dense_w1"][i], w["dense_w3"][i], w["dense_w2"][i],
                              w["final_norm_g"], last)
        else:
            x2d = _moe(x2d, att, ret, w["w_out"][layer], w["ffn_norm_g"][layer][None, :],
                       w["router_pad"][i], w["moe_w1"][i], w["moe_w3"][i], w["moe_w2"][i],
                       w["final_norm_g"], last)
    return x2d.reshape(batch, seq, D_MODEL)


def kernel(x_prompt, x_sample, attn_norm_g, w_in, attn_out_g, ret_out_g, ret_decay_fwd, ret_decay_bwd, w_out, ffn_norm_g, dense_w1, dense_w3, dense_w2, moe_router, moe_w1, moe_w3, moe_w2, final_norm_g):
    lg_f = jnp.log1p(-jnp.exp(ret_decay_fwd.astype(F32)))
    lg_b = jnp.log1p(-jnp.exp(ret_decay_bwd.astype(F32)))
    w = {
        "attn_norm_g": attn_norm_g, "attn_out_g": attn_out_g, "ret_out_g": ret_out_g,
        "ffn_norm_g": ffn_norm_g, "final_norm_g": final_norm_g[None, :],
        "lg": jnp.stack([lg_f, lg_b], axis=1),
        "w_in": w_in.astype(BF16), "w_out": w_out.astype(BF16),
        "dense_w1": dense_w1.astype(BF16), "dense_w3": dense_w3.astype(BF16),
        "dense_w2": dense_w2.astype(BF16),
        "moe_w1": moe_w1.astype(BF16), "moe_w3": moe_w3.astype(BF16), "moe_w2": moe_w2.astype(BF16),
        "router_pad": _pack_router(moe_router.astype(F32)),
    }
    return _trunk(x_prompt, w), _trunk(x_sample, w)
```

```python
import functools
import math

import jax
import jax.numpy as jnp
from jax import lax
from jax.experimental import pallas as pl
from jax.experimental.pallas import tpu as pltpu

F32 = jnp.float32
BF16 = jnp.bfloat16

D_MODEL = 1024
N_ATTN_HEADS = 8
ATTN_HEAD_DIM = 64
ATTN_WIDTH = N_ATTN_HEADS * ATTN_HEAD_DIM
ATTN_RADIUS = 64
DILATIONS = (1, 4, 16)
ROPE_THETA = 500000.0
ROPE_DIM = ATTN_HEAD_DIM // 4
N_RET_HEADS = 4
RET_HEAD_DIM = 128
RET_WIDTH = N_RET_HEADS * RET_HEAD_DIM
RET_THETA = 10000.0
RET_CHUNK = 128
IN_COLS = 3 * ATTN_WIDTH + 4 * RET_WIDTH
N_EXPERTS = 8
D_FF = 2816
NORM_EPS = 1e-6
NEG_INF = -1e30
TINY = 1e-30

LANES = 128
N_COL_BLOCKS = IN_COLS // LANES
VMEM_LIMIT = 56 * 1024 * 1024

TM = 512
TME = 512
FF_CHUNKS = ((0, 1024), (1024, 2048), (2048, 2816))
ATT_BQ = 128
ATT_W = ATT_BQ + 2 * ATTN_RADIUS
ATT_NB = 8
RET_UNROLL = 8
DMA_UNROLL = 8


def _cparams(sem):
    return pltpu.CompilerParams(dimension_semantics=sem, vmem_limit_bytes=VMEM_LIMIT)


def _rms(x, g):
    return x * lax.rsqrt(jnp.mean(x * x, axis=-1, keepdims=True) + NORM_EPS) * g


def _silu(x):
    return x / (1.0 + jnp.exp(-x))


def _in_proj_body(x_ref, g_ref, w_ref, ca_ref, sa1_ref, sa2_ref, cr_ref, sr_ref, o_ref):
    h = _rms(x_ref[...], g_ref[...]).astype(BF16)
    for c in range(7):
        acc = jnp.dot(h, w_ref[:, c * 512:(c + 1) * 512], preferred_element_type=F32)
        for s in range(4):
            a = acc[:, s * LANES:(s + 1) * LANES]
            if c in (0, 1):
                a = (a * ca_ref[...] + pltpu.roll(a, ROPE_DIM // 2, 1) * sa1_ref[...]
                     + pltpu.roll(a, LANES - ROPE_DIM // 2, 1) * sa2_ref[...])
                if c == 0:
                    a = a * (ATTN_HEAD_DIM ** -0.5)
            elif c in (3, 4):
                a = a * cr_ref[...] + pltpu.roll(a, RET_HEAD_DIM // 2, 1) * sr_ref[...]
                if c == 4:
                    a = a * (RET_HEAD_DIM ** -0.5)
            o_ref[c * 4 + s] = a.astype(BF16)


def _in_proj(x2d, g, w, tabs, seq):
    t = x2d.shape[0]
    nblk = seq // TM
    tab_spec = pl.BlockSpec((TM, LANES), lambda i: (i % nblk, 0))
    return pl.pallas_call(
        _in_proj_body,
        grid=(t // TM,),
        in_specs=[pl.BlockSpec((TM, D_MODEL), lambda i: (i, 0)),
                  pl.BlockSpec((1, D_MODEL), lambda i: (0, 0)),
                  pl.BlockSpec((D_MODEL, IN_COLS), lambda i: (0, 0)),
                  tab_spec, tab_spec, tab_spec, tab_spec, tab_spec],
        out_specs=pl.BlockSpec((N_COL_BLOCKS, TM, LANES), lambda i: (0, i, 0)),
        out_shape=jax.ShapeDtypeStruct((N_COL_BLOCKS, t, LANES), BF16),
        compiler_params=_cparams(("arbitrary",)),
        name="in_proj",
    )(x2d, g, w, *tabs)


def _attn_body(q_ref, k_ref, v_ref, g_ref, o_ref,
               k1, v1, q4, k4, v4, q16, k16, v16, bias,
               acc_a, ms_a, ls_a, acc_b, ms_b, ls_b, *, seq):
    st_a = (acc_a, ms_a, ls_a)
    st_b = (acc_b, ms_b, ls_b)
    stage = 512
    pad = ATTN_RADIUS
    lane = lax.broadcasted_iota(jnp.int32, (ATT_BQ, LANES), 1)
    head0 = lane < ATTN_HEAD_DIM
    khead0 = lax.broadcasted_iota(jnp.int32, (ATT_W, LANES), 1) < ATTN_HEAD_DIM

    qi = lax.broadcasted_iota(jnp.int32, (ATT_BQ, ATT_W), 0)
    kj = lax.broadcasted_iota(jnp.int32, (ATT_BQ, ATT_W), 1)
    band = jnp.abs(kj - pad - qi) <= ATTN_RADIUS
    for e in range(4):
        ok = band
        if e & 1:
            ok = ok & (kj >= pad)
        if e & 2:
            ok = ok & (kj < ATT_BQ + pad)
        bias[e] = jnp.where(ok, 0.0, NEG_INF)

    for arr in (k1, v1, k4, v4, k16, v16):
        arr[pl.ds(0, pad), :] = jnp.zeros((pad, LANES), BF16)
        arr[pl.ds(seq + pad, pad), :] = jnp.zeros((pad, LANES), BF16)

    cls4, cls16 = seq // 4, seq // 16
    xf, x4f = st_a[0], st_b[0]
    for src, d1, d4, d16 in ((q_ref, None, q4, q16), (k_ref, k1, k4, k16), (v_ref, v1, v4, v16)):
        off = 0 if d1 is None else pad

        def upcast(i, c, src=src, d1=d1):
            r0 = pl.multiple_of(i * stage, stage)
            x = src[pl.ds(r0, stage), :]
            xf[pl.ds(r0, stage), :] = x.astype(F32)
            if d1 is not None:
                d1[pl.ds(pad + r0, stage), :] = x
            return c
        lax.fori_loop(0, seq // stage, upcast, 0)
        ch = min(cls4, 256)
        for a in range(4):
            def to4(i, c, d4=d4, a=a, ch=ch, off=off):
                j0 = pl.multiple_of(i * ch, ch)
                x = xf[pl.ds(a + 4 * j0, ch, stride=4), :]
                x4f[pl.ds(a * cls4 + j0, ch), :] = x
                d4[pl.ds(off + a * cls4 + j0, ch), :] = x.astype(BF16)
                return c
            lax.fori_loop(0, cls4 // ch, to4, 0)
        ch = min(cls16, 256)
        for a in range(4):
            for b in range(4):
                def to16(i, c, d16=d16, a=a, b=b, ch=ch, off=off):
                    i0 = pl.multiple_of(i * ch, ch)
                    d16[pl.ds(off + a * cls4 + b * cls16 + i0, ch), :] = (
                        x4f[pl.ds(a * cls4 + b + 4 * i0, ch, stride=4), :].astype(BF16))
                    return c
                lax.fori_loop(0, cls16 // ch, to16, 0)

    def relayout_16_to_4(src, dst):
        ch = min(cls16, ATT_BQ)
        for a in range(4):
            for b in range(4):
                def body(i, c, a=a, b=b):
                    i0 = pl.multiple_of(i * ch, ch)
                    for s_, d_ in zip(src, dst):
                        d_[pl.ds(a * cls4 + b + 4 * i0, ch, stride=4), :] = (
                            s_[pl.ds(a * cls4 + b * cls16 + i0, ch), :])
                    return c
                lax.fori_loop(0, cls16 // ch, body, 0)

    def relayout_4_to_1(src, dst):
        ch = ATT_BQ
        for a in range(4):
            def body(i, c, a=a):
                j0 = pl.multiple_of(i * ch, ch)
                for s_, d_ in zip(src, dst):
                    d_[pl.ds(a + 4 * j0, ch, stride=4), :] = s_[pl.ds(a * cls4 + j0, ch), :]
                return c
            lax.fori_loop(0, cls4 // ch, body, 0)

    gain = g_ref[...]

    def branch(d, qd, kd, vd, state, mode):
        acc, ms, ls = state
        nblk = seq // d // ATT_BQ

        def group(g, c):
            blocks = []
            for u in range(ATT_NB):
                t = g * ATT_NB + u
                base = pl.multiple_of(t * ATT_BQ, ATT_BQ)
                n = t & (nblk - 1)
                edge = jnp.where(n == 0, 1, 0) + jnp.where(n == nblk - 1, 2, 0)
                blocks.append((base, edge))
            scores = []
            for base, edge in blocks:
                qb = qd[pl.ds(base, ATT_BQ), :]
                kw = kd[pl.ds(base, ATT_W), :]
                for hd in range(2):
                    qh = jnp.where(head0 if hd == 0 else ~head0, qb, jnp.zeros_like(qb))
                    scores.append(lax.dot_general(qh, kw, (((1,), (1,)), ((), ())),
                                                  preferred_element_type=F32) + bias[edge])
            maxes = [jnp.max(s, axis=1, keepdims=True) for s in scores]
            probs = [jnp.exp(s - m).astype(BF16) for s, m in zip(scores, maxes)]
            fulls = []
            for i, (base, edge) in enumerate(blocks):
                vw = vd[pl.ds(base, ATT_W), :]
                for hd in range(2):
                    vh = jnp.where(khead0 if hd == 0 else ~khead0, vw, jnp.ones_like(vw))
                    fulls.append(jnp.dot(probs[2 * i + hd], vh, preferred_element_type=F32))
            for i, (base, edge) in enumerate(blocks):
                f0, f1 = fulls[2 * i], fulls[2 * i + 1]
                o_new = jnp.where(head0, f0, f1)
                l_new = pltpu.roll(jnp.where(head0, f1, f0), ATTN_HEAD_DIM, 1)
                l_new = jnp.maximum(l_new, TINY)
                m_new = jnp.where(head0, maxes[2 * i], maxes[2 * i + 1])
                rows = pl.ds(base, ATT_BQ)
                if mode == "first":
                    acc[rows, :] = o_new
                    ms[rows, :] = m_new
                    ls[rows, :] = l_new
                    continue
                m_old = ms[rows, :]
                m_tot = jnp.maximum(m_old, m_new)
                a_old = jnp.exp(m_old - m_tot)
                a_new = jnp.exp(m_new - m_tot)
                o_tot = acc[rows, :] * a_old + o_new * a_new
                l_tot = ls[rows, :] * a_old + l_new * a_new
                if mode == "merge":
                    acc[rows, :] = o_tot
                    ls[rows, :] = l_tot
                    ms[rows, :] = m_tot
                else:
                    y = o_tot / l_tot
                    sq = y * y
                    s0 = jnp.sum(jnp.where(head0, sq, 0.0), axis=1, keepdims=True)
                    s1 = jnp.sum(jnp.where(head0, 0.0, sq), axis=1, keepdims=True)
                    mean = jnp.where(head0, s0, s1) * (1.0 / ATTN_HEAD_DIM)
                    o_ref[rows, :] = (y * lax.rsqrt(mean + NORM_EPS) * gain).astype(BF16)
            return c

        lax.fori_loop(0, seq // (ATT_BQ * ATT_NB), group, 0)

    branch(16, q16, k16, v16, st_a, "first")
    relayout_16_to_4(st_a, st_b)
    branch(4, q4, k4, v4, st_b, "merge")
    relayout_4_to_1(st_b, st_a)
    branch(1, q_ref, k1, v1, st_a, "last")


def _attention(proj, gain, batch, seq):
    t = batch * seq
    padded = seq + 2 * ATTN_RADIUS
    qkv_spec = lambda off: pl.BlockSpec((None, seq, LANES), lambda b, hp: (off + hp, b, 0),
                                        pipeline_mode=pl.Buffered(1))
    return pl.pallas_call(
        functools.partial(_attn_body, seq=seq),
        grid=(batch, ATTN_WIDTH // LANES),
        in_specs=[qkv_spec(0), qkv_spec(4), qkv_spec(8),
                  pl.BlockSpec((None, 1, LANES), lambda b, hp: (hp, 0, 0))],
        out_specs=pl.BlockSpec((seq, LANES), lambda b, hp: (b, hp)),
        out_shape=jax.ShapeDtypeStruct((t, ATTN_WIDTH), BF16),
        scratch_shapes=[pltpu.VMEM((padded, LANES), BF16), pltpu.VMEM((padded, LANES), BF16),
                        pltpu.VMEM((seq, LANES), BF16),
                        pltpu.VMEM((padded, LANES), BF16), pltpu.VMEM((padded, LANES), BF16),
                        pltpu.VMEM((seq, LANES), BF16),
                        pltpu.VMEM((padded, LANES), BF16), pltpu.VMEM((padded, LANES), BF16),
                        pltpu.VMEM((4, ATT_BQ, ATT_W), F32)]
        + [pltpu.VMEM((seq, LANES), F32)] * 6,
        compiler_params=_cparams(("arbitrary", "arbitrary")),
        name="dilated_attn",
    )(proj, proj, proj, gain)


def _ret_body(lg_ref, q_ref, k_ref, v_ref, gate_ref, gain_ref, o_ref, ob, dec, *, seq):
    c_len = RET_CHUNK
    n_chunks = seq // c_len
    hd = pl.program_id(1)
    lf = lg_ref[0, hd]
    lb = lg_ref[1, hd]
    ri = lax.broadcasted_iota(jnp.int32, (c_len, c_len), 0).astype(F32)
    ci = lax.broadcasted_iota(jnp.int32, (c_len, c_len), 1).astype(F32)
    diff = ri - ci
    dec[0] = jnp.where(diff >= 0, jnp.exp(lf * jnp.maximum(diff, 0.0)),
                       jnp.exp(lb * jnp.maximum(-diff, 0.0)))
    dec[1] = jnp.exp(lf * (ri + 1.0))
    dec[2] = jnp.exp(lf * (c_len - 1.0 - ri))
    dec[3] = jnp.exp(lb * (c_len - ri))
    dec[4] = jnp.exp(lb * ri)
    cdf = jnp.exp(lf * c_len)
    cdb = jnp.exp(lb * c_len)
    tdot = (((0,), (0,)), ((), ()))

    def bwd(jg, state):
        for u in range(RET_UNROLL):
            j = jg * RET_UNROLL + u
            rows = pl.ds(pl.multiple_of((n_chunks - 1 - j) * c_len, c_len), c_len)
            qn = q_ref[rows, :].astype(F32)
            kn = k_ref[rows, :].astype(F32)
            ob[rows, :] = jnp.dot((qn * dec[3]).astype(BF16), state.astype(BF16),
                                  preferred_element_type=F32)
            kv = lax.dot_general((kn * dec[4]).astype(BF16), v_ref[rows, :], tdot,
                                 preferred_element_type=F32)
            state = state * cdb + kv
        return state

    lax.fori_loop(0, n_chunks // RET_UNROLL, bwd, jnp.zeros((c_len, c_len), F32))

    gain = gain_ref[...]

    def fwd(jg, state):
        row_sets = [pl.ds(pl.multiple_of((jg * RET_UNROLL + u) * c_len, c_len), c_len)
                    for u in range(RET_UNROLL)]
        states = []
        for rows in row_sets:
            states.append(state)
            kn = k_ref[rows, :].astype(F32)
            kv = lax.dot_general((kn * dec[2]).astype(BF16), v_ref[rows, :], tdot,
                                 preferred_element_type=F32)
            state = state * cdf + kv
        for rows, st in zip(row_sets, states):
            qb = q_ref[rows, :]
            vb = v_ref[rows, :]
            sc = lax.dot_general(qb, k_ref[rows, :], (((1,), (1,)), ((), ())),
                                 preferred_element_type=F32) * dec[0]
            tot = (ob[rows, :] + jnp.dot(sc.astype(BF16), vb, preferred_element_type=F32)
                   + jnp.dot((qb.astype(F32) * dec[1]).astype(BF16), st.astype(BF16),
                             preferred_element_type=F32))
            y = tot - jnp.mean(tot, axis=-1, keepdims=True)
            y = y * lax.rsqrt(jnp.mean(y * y, axis=-1, keepdims=True) + NORM_EPS) * gain
            o_ref[rows, :] = (y * _silu(gate_ref[rows, :].astype(F32))).astype(BF16)
        return state

    lax.fori_loop(0, n_chunks // RET_UNROLL, fwd, jnp.zeros((c_len, c_len), F32))


def _retention(proj, gain, lg, batch, seq):
    t = batch * seq
    spec = lambda off: pl.BlockSpec((None, seq, LANES), lambda b, hd: (off + hd, b, 0))
    return pl.pallas_call(
        functools.partial(_ret_body, seq=seq),
        grid=(batch, N_RET_HEADS),
        in_specs=[pl.BlockSpec(memory_space=pltpu.SMEM),
                  spec(12), spec(16), spec(20), spec(24),
                  pl.BlockSpec((None, 1, LANES), lambda b, hd: (hd, 0, 0))],
        out_specs=pl.BlockSpec((seq, LANES), lambda b, hd: (b, hd)),
        out_shape=jax.ShapeDtypeStruct((t, RET_WIDTH), BF16),
        scratch_shapes=[pltpu.VMEM((seq, LANES), F32),
                        pltpu.VMEM((5, RET_CHUNK, RET_CHUNK), F32)],
        compiler_params=_cparams(("arbitrary", "arbitrary")),
        name="retention",
    )(lg, proj, proj, proj, proj, gain)


def _post_dense_body(x_ref, a_ref, r_ref, wo_ref, g_ref, w1_ref, w3_ref, w2_ref, fg_ref, o_ref,
                     *, final_norm):
    x2 = (x_ref[...]
          + jnp.dot(a_ref[...], wo_ref[0:ATTN_WIDTH, :], preferred_element_type=F32)
          + jnp.dot(r_ref[...], wo_ref[ATTN_WIDTH:, :], preferred_element_type=F32))
    h = _rms(x2, g_ref[...]).astype(BF16)
    ffn = None
    for c0, c1 in FF_CHUNKS:
        a = jnp.dot(h, w1_ref[:, c0:c1], preferred_element_type=F32)
        b = jnp.dot(h, w3_ref[:, c0:c1], preferred_element_type=F32)
        u = (_silu(a) * b).astype(BF16)
        d = jnp.dot(u, w2_ref[c0:c1, :], preferred_element_type=F32)
        ffn = d if ffn is None else ffn + d
    y = x2 + ffn
    if final_norm:
        y = _rms(y, fg_ref[...])
    o_ref[...] = y


def _post_dense(x2d, att, ret, wo, g, w1, w3, w2, fg, final_norm):
    t = x2d.shape[0]
    full = lambda a: pl.BlockSpec(a.shape, lambda i: (0,) * a.ndim)
    return pl.pallas_call(
        functools.partial(_post_dense_body, final_norm=final_norm),
        grid=(t // TM,),
        in_specs=[pl.BlockSpec((TM, D_MODEL), lambda i: (i, 0)),
                  pl.BlockSpec((TM, ATTN_WIDTH), lambda i: (i, 0)),
                  pl.BlockSpec((TM, RET_WIDTH), lambda i: (i, 0)),
                  full(wo), full(g), full(w1), full(w3), full(w2), full(fg)],
        out_specs=pl.BlockSpec((TM, D_MODEL), lambda i: (i, 0)),
        out_shape=jax.ShapeDtypeStruct((t, D_MODEL), F32),
        compiler_params=_cparams(("arbitrary",)),
        name="post_dense",
    )(x2d, att, ret, wo, g, w1, w3, w2, fg)


def _post_route_body(x_ref, a_ref, r_ref, wo_ref, g_ref, rt_ref,
                     x2_ref, h_ref, ri_ref, rf_ref, cnt_ref, base):
    @pl.when(pl.program_id(0) == 0)
    def _():
        base[...] = jnp.zeros_like(base)

    x2 = (x_ref[...]
          + jnp.dot(a_ref[...], wo_ref[0:ATTN_WIDTH, :], preferred_element_type=F32)
          + jnp.dot(r_ref[...], wo_ref[ATTN_WIDTH:, :], preferred_element_type=F32))
    x2_ref[...] = x2
    h = _rms(x2, g_ref[...])
    h_ref[...] = h
    hi = h.astype(BF16)
    lo = (h - hi.astype(F32)).astype(BF16)
    parts = (jnp.dot(hi, rt_ref[...], preferred_element_type=F32)
             + jnp.dot(lo, rt_ref[...], preferred_element_type=F32))
    logits = parts + pltpu.roll(parts, LANES - N_EXPERTS, 1)
    lane = lax.broadcasted_iota(jnp.int32, (TM, LANES), 1).astype(F32)
    lg = jnp.where(lane < N_EXPERTS, logits, -jnp.inf)
    m1 = jnp.max(lg, axis=1, keepdims=True)
    i1 = jnp.min(jnp.where(lg == m1, lane, float(LANES)), axis=1, keepdims=True)
    lg2 = jnp.where(lane == i1, -jnp.inf, lg)
    m2 = jnp.max(lg2, axis=1, keepdims=True)
    i2 = jnp.min(jnp.where(lg2 == m2, lane, float(LANES)), axis=1, keepdims=True)
    e = jnp.exp(m2 - m1)
    g1 = 1.0 / (1.0 + e)
    g2 = e / (1.0 + e)
    sel1 = lane == i1
    sel2 = lane == i2
    onehot = jnp.where(sel1 | sel2, 1.0, 0.0)
    row = lax.broadcasted_iota(jnp.int32, (TM, TM), 0)
    col = lax.broadcasted_iota(jnp.int32, (TM, TM), 1)
    earlier = jnp.where(col < row, 1.0, 0.0).astype(BF16)
    cnt = jnp.dot(earlier, onehot.astype(BF16), preferred_element_type=F32) + base[...]
    rank1 = jnp.sum(jnp.where(sel1, cnt, 0.0), axis=1, keepdims=True)
    rank2 = jnp.sum(jnp.where(sel2, cnt, 0.0), axis=1, keepdims=True)
    new_base = base[...] + jnp.sum(onehot, axis=0, keepdims=True)
    base[...] = new_base
    cnt_ref[...] = jnp.broadcast_to(new_base, cnt_ref.shape)
    ri = jnp.where(lane == 0, i1, jnp.where(lane == 1, i2,
                   jnp.where(lane == 2, rank1, jnp.where(lane == 3, rank2, 0.0))))
    ri_ref[...] = ri.astype(jnp.int32)
    rf_ref[...] = jnp.where(lane == 0, g1, jnp.where(lane == 1, g2, 0.0))


def _post_route(x2d, att, ret, wo, g, router_pad):
    t = x2d.shape[0]
    full = lambda a: pl.BlockSpec(a.shape, lambda i: (0,) * a.ndim)
    row = lambda w: pl.BlockSpec((TM, w), lambda i: (i, 0))
    return pl.pallas_call(
        _post_route_body,
        grid=(t // TM,),
        in_specs=[row(D_MODEL), row(ATTN_WIDTH), row(RET_WIDTH),
                  full(wo), full(g), full(router_pad)],
        out_specs=[row(D_MODEL), row(D_MODEL), row(LANES), row(LANES),
                   pl.BlockSpec((8, LANES), lambda i: (0, 0))],
        out_shape=[jax.ShapeDtypeStruct((t, D_MODEL), F32),
                   jax.ShapeDtypeStruct((t, D_MODEL), F32),
                   jax.ShapeDtypeStruct((t, LANES), jnp.int32),
                   jax.ShapeDtypeStruct((t, LANES), F32),
                   jax.ShapeDtypeStruct((8, LANES), F32)],
        scratch_shapes=[pltpu.VMEM((1, LANES), F32)],
        compiler_params=_cparams(("arbitrary",)),
        name="post_route",
    )(x2d, att, ret, wo, g, router_pad)


def _row_copy_wait(src, dst, sem):
    pltpu.make_async_copy(src, dst, sem).wait()


def _scatter_body(ginfo, pos_hbm, h_ref, o_hbm, idx, zbuf, sem_i, sem_r, sem_z, *, n_sorted):
    i = pl.program_id(0)

    @pl.when(i == 0)
    def _():
        zbuf[...] = jnp.zeros_like(zbuf)
        total = ginfo[N_EXPERTS - 1]
        fills = []
        for e in range(N_EXPERTS):
            start = pl.multiple_of(jnp.maximum(ginfo[e] - TME, 0), TME)
            fills.append((ginfo[N_EXPERTS + e] > 0, start))
        for j in range(N_EXPERTS):
            start = n_sorted - (j + 1) * TME
            fills.append((start >= total, start))
        for phase in ("start", "wait"):
            for cond, start in fills:
                @pl.when(cond)
                def _(start=start, phase=phase):
                    cp = pltpu.make_async_copy(zbuf, o_hbm.at[pl.ds(start, TME), :], sem_z)
                    cp.start() if phase == "start" else cp.wait()

    cp = pltpu.make_async_copy(pos_hbm.at[i], idx, sem_i)
    cp.start()
    cp.wait()

    def issue(t, c):
        src = h_ref.at[pl.ds(t, 1), :]
        pltpu.make_async_copy(src, o_hbm.at[pl.ds(idx[t], 1), :], sem_r).start()
        pltpu.make_async_copy(src, o_hbm.at[pl.ds(idx[TM + t], 1), :], sem_r).start()
        return c

    lax.fori_loop(0, TM, issue, 0, unroll=DMA_UNROLL)

    def drain(t, c):
        src = h_ref.at[pl.ds(t, 1), :]
        _row_copy_wait(src, o_hbm.at[pl.ds(idx[t], 1), :], sem_r)
        _row_copy_wait(src, o_hbm.at[pl.ds(idx[TM + t], 1), :], sem_r)
        return c

    lax.fori_loop(0, TM, drain, 0, unroll=DMA_UNROLL)


def _scatter(ginfo, pos, h, n_sorted):
    t = h.shape[0]
    return pl.pallas_call(
        functools.partial(_scatter_body, n_sorted=n_sorted),
        grid=(t // TM,),
        in_specs=[pl.BlockSpec(memory_space=pltpu.SMEM),
                  pl.BlockSpec(memory_space=pl.ANY),
                  pl.BlockSpec((TM, D_MODEL), lambda i: (i, 0))],
        out_specs=pl.BlockSpec(memory_space=pl.ANY),
        out_shape=jax.ShapeDtypeStruct((n_sorted, D_MODEL), F32),
        scratch_shapes=[pltpu.SMEM((2 * TM,), jnp.int32),
                        pltpu.VMEM((TME, D_MODEL), F32),
                        pltpu.SemaphoreType.DMA, pltpu.SemaphoreType.DMA,
                        pltpu.SemaphoreType.DMA],
        compiler_params=_cparams(("arbitrary",)),
        name="moe_dispatch",
    )(ginfo, pos, h)


def _expert_body(te_ref, nv_ref, h_ref, w1_ref, w3_ref, w2_ref, o_ref):
    nv = nv_ref[pl.program_id(0)]

    @pl.when(nv == 0)
    def _():
        o_ref[...] = jnp.zeros_like(o_ref)

    @pl.when(nv > 0)
    def _():
        rows = lax.broadcasted_iota(jnp.int32, (TME, D_MODEL), 0)
        h = jnp.where(rows < nv, h_ref[...], 0.0).astype(BF16)
        ffn = None
        for c0, c1 in FF_CHUNKS:
            a = jnp.dot(h, w1_ref[:, c0:c1], preferred_element_type=F32)
            b = jnp.dot(h, w3_ref[:, c0:c1], preferred_element_type=F32)
            u = (_silu(a) * b).astype(BF16)
            d = jnp.dot(u, w2_ref[c0:c1, :], preferred_element_type=F32)
            ffn = d if ffn is None else ffn + d
        o_ref[...] = ffn


def _experts(tile_e, tile_nv, hs, w1, w3, w2):
    n_sorted = hs.shape[0]
    grid_spec = pltpu.PrefetchScalarGridSpec(
        num_scalar_prefetch=2,
        grid=(n_sorted // TME,),
        in_specs=[pl.BlockSpec((TME, D_MODEL), lambda j, te, nv: (j, 0)),
                  pl.BlockSpec((None, D_MODEL, D_FF), lambda j, te, nv: (te[j], 0, 0)),
                  pl.BlockSpec((None, D_MODEL, D_FF), lambda j, te, nv: (te[j], 0, 0)),
                  pl.BlockSpec((None, D_FF, D_MODEL), lambda j, te, nv: (te[j], 0, 0))],
        out_specs=pl.BlockSpec((TME, D_MODEL), lambda j, te, nv: (j, 0)),
    )
    return pl.pallas_call(
        _expert_body,
        grid_spec=grid_spec,
        out_shape=jax.ShapeDtypeStruct((n_sorted, D_MODEL), F32),
        compiler_params=_cparams(("arbitrary",)),
        name="moe_experts",
    )(tile_e, tile_nv, hs, w1, w3, w2)


def _combine_body(pos_hbm, x_ref, rf_ref, fg_ref, y_hbm, o_ref, idx, b1, b2, sem_i, sem_r,
                  *, final_norm):
    i = pl.program_id(0)
    cp = pltpu.make_async_copy(pos_hbm.at[i], idx, sem_i)
    cp.start()
    cp.wait()

    def issue(t, c):
        pltpu.make_async_copy(y_hbm.at[pl.ds(idx[t], 1), :], b1.at[pl.ds(t, 1), :], sem_r).start()
        pltpu.make_async_copy(y_hbm.at[pl.ds(idx[TM + t], 1), :], b2.at[pl.ds(t, 1), :], sem_r).start()
        return c

    lax.fori_loop(0, TM, issue, 0, unroll=DMA_UNROLL)

    def drain(t, c):
        _row_copy_wait(y_hbm.at[pl.ds(idx[t], 1), :], b1.at[pl.ds(t, 1), :], sem_r)
        _row_copy_wait(y_hbm.at[pl.ds(idx[TM + t], 1), :], b2.at[pl.ds(t, 1), :], sem_r)
        return c

    lax.fori_loop(0, TM, drain, 0, unroll=DMA_UNROLL)

    g1 = rf_ref[:, 0:1]
    g2 = rf_ref[:, 1:2]
    y = x_ref[...] + (g1 * b1[...] + g2 * b2[...])
    if final_norm:
        y = _rms(y, fg_ref[...])
    o_ref[...] = y


def _combine(pos, x2, rf, fg, ys, final_norm):
    t = x2.shape[0]
    return pl.pallas_call(
        functools.partial(_combine_body, final_norm=final_norm),
        grid=(t // TM,),
        in_specs=[pl.BlockSpec(memory_space=pl.ANY),
                  pl.BlockSpec((TM, D_MODEL), lambda i: (i, 0)),
                  pl.BlockSpec((TM, LANES), lambda i: (i, 0)),
                  pl.BlockSpec((1, D_MODEL), lambda i: (0, 0)),
                  pl.BlockSpec(memory_space=pl.ANY)],
        out_specs=pl.BlockSpec((TM, D_MODEL), lambda i: (i, 0)),
        out_shape=jax.ShapeDtypeStruct((t, D_MODEL), F32),
        scratch_shapes=[pltpu.SMEM((2 * TM,), jnp.int32),
                        pltpu.VMEM((TM, D_MODEL), F32), pltpu.VMEM((TM, D_MODEL), F32),
                        pltpu.SemaphoreType.DMA, pltpu.SemaphoreType.DMA],
        compiler_params=_cparams(("arbitrary",)),
        name="moe_combine",
    )(pos, x2, rf, fg, ys)


def _moe(x2d, att, ret, wo, g, router_pad, w1, w3, w2, fg, final_norm):
    t = x2d.shape[0]
    n_sorted = 2 * t + N_EXPERTS * TME
    n_tiles = n_sorted // TME
    x2, h, ri, rf, cnt = _post_route(x2d, att, ret, wo, g, router_pad)

    counts = cnt[0, :N_EXPERTS].astype(jnp.int32)
    padded = ((counts + TME - 1) // TME) * TME
    ends = jnp.cumsum(padded)
    starts = ends - padded
    experts = jnp.arange(N_EXPERTS, dtype=jnp.int32)
    start_of = lambda e: jnp.sum(jnp.where(e[:, None] == experts[None, :], starts[None, :], 0), axis=1)
    pos1 = start_of(ri[:, 0]) + ri[:, 2]
    pos2 = start_of(ri[:, 1]) + ri[:, 3]
    pos = jnp.concatenate([pos1.reshape(t // TM, TM), pos2.reshape(t // TM, TM)], axis=1)
    tile_start = jnp.arange(n_tiles, dtype=jnp.int32) * TME
    tile_e = jnp.minimum(jnp.sum(tile_start[:, None] >= ends[None, :], axis=1), N_EXPERTS - 1)
    tile_e = tile_e.astype(jnp.int32)
    tile_nv = jnp.clip(starts[tile_e] + counts[tile_e] - tile_start, 0, TME)
    tile_nv = jnp.where(tile_start < ends[-1], tile_nv, 0).astype(jnp.int32)
    ginfo = jnp.concatenate([ends, padded]).astype(jnp.int32)

    hs = _scatter(ginfo, pos, h, n_sorted)
    ys = _experts(tile_e, tile_nv, hs, w1, w3, w2)
    return _combine(pos, x2, rf, fg, ys, final_norm)


def _rope_tables(seq):
    pos = jnp.arange(seq, dtype=F32)[:, None]
    lane = jnp.arange(LANES)
    j = lane % ATTN_HEAD_DIM
    half = ROPE_DIM // 2
    inv_a = ROPE_THETA ** (-jnp.arange(half, dtype=F32) * 2.0 / ROPE_DIM)
    ang = pos * inv_a[None, :]
    cos_a, sin_a = jnp.cos(ang), jnp.sin(ang)
    jc = jnp.where(j < half, j, j - half) % half
    ca = jnp.where(j[None, :] < ROPE_DIM, cos_a[:, jc], 1.0)
    sa1 = jnp.where(((j >= half) & (j < ROPE_DIM))[None, :], sin_a[:, jc], 0.0)
    sa2 = jnp.where((j < half)[None, :], -sin_a[:, jc], 0.0)
    half_r = RET_HEAD_DIM // 2
    inv_r = RET_THETA ** (-jnp.arange(half_r, dtype=F32) * 2.0 / RET_HEAD_DIM)
    ang_r = pos * inv_r[None, :]
    cos_r, sin_r = jnp.cos(ang_r), jnp.sin(ang_r)
    cr = cos_r[:, lane % half_r]
    sr = jnp.where((lane < half_r)[None, :], -sin_r[:, lane % half_r], sin_r[:, lane % half_r])
    return ca, sa1, sa2, cr, sr


def _pack_router(router):
    hi = router.astype(BF16)
    lo = (router - hi.astype(F32)).astype(BF16)
    packed = jnp.concatenate([hi, lo], axis=-1)
    return jnp.pad(packed, ((0, 0), (0, 0), (0, LANES - 2 * N_EXPERTS)))


def _trunk(x, w):
    batch, seq, _ = x.shape
    t = batch * seq
    assert seq % (16 * ATT_BQ) == 0 and seq % TM == 0 and t % TM == 0
    tabs = _rope_tables(seq)
    x2d = x.reshape(t, D_MODEL)
    depth = w["w_in"].shape[0]
    for layer in range(depth):
        last = layer == depth - 1
        proj = _in_proj(x2d, w["attn_norm_g"][layer][None, :], w["w_in"][layer], tabs, seq)
        att = _attention(proj, w["attn_out_g"][layer].reshape(ATTN_WIDTH // LANES, 1, LANES), batch, seq)
        ret = _retention(proj, w["ret_out_g"][layer].reshape(N_RET_HEADS, 1, LANES),
                         w["lg"][layer], batch, seq)
        i = layer // 2
        if layer % 2 == 0:
            x2d = _post_dense(x2d, att, ret, w["w_out"][layer], w["ffn_norm_g"][layer][None, :],
                              w["dense_w1"][i], w["dense_w3"][i], w["dense_w2"][i],
                              w["final_norm_g"], last)
        else:
            x2d = _moe(x2d, att, ret, w["w_out"][layer], w["ffn_norm_g"][layer][None, :],
                       w["router_pad"][i], w["moe_w1"][i], w["moe_w3"][i], w["moe_w2"][i],
                       w["final_norm_g"], last)
    return x2d.reshape(batch, seq, D_MODEL)


def kernel(x_prompt, x_sample, attn_norm_g, w_in, attn_out_g, ret_out_g, ret_decay_fwd, ret_decay_bwd, w_out, ffn_norm_g, dense_w1, dense_w3, dense_w2, moe_router, moe_w1, moe_w3, moe_w2, final_norm_g):
    lg_f = jnp.log1p(-jnp.exp(ret_decay_fwd.astype(F32)))
    lg_b = jnp.log1p(-jnp.exp(ret_decay_bwd.astype(F32)))
    w = {
        "attn_norm_g": attn_norm_g, "attn_out_g": attn_out_g, "ret_out_g": ret_out_g,
        "ffn_norm_g": ffn_norm_g, "final_norm_g": final_norm_g[None, :],
        "lg": jnp.stack([lg_f, lg_b], axis=1),
        "w_in": w_in.astype(BF16), "w_out": w_out.astype(BF16),
        "dense_w1": dense_w1.astype(BF16), "dense_w3": dense_w3.astype(BF16),
        "dense_w2": dense_w2.astype(BF16),
        "moe_w1": moe_w1.astype(BF16), "moe_w3": moe_w3.astype(BF16), "moe_w2": moe_w2.astype(BF16),
        "router_pad": _pack_router(moe_router.astype(F32)),
    }
    return _trunk(x_prompt, w), _trunk(x_sample, w)
```

```python
import functools

import jax
import jax.numpy as jnp
from jax import lax
from jax.experimental import pallas as pl
from jax.experimental.pallas import tpu as pltpu

F32 = jnp.float32
BF16 = jnp.bfloat16

D_MODEL = 1024
N_ATTN_HEADS = 8
ATTN_HEAD_DIM = 64
ATTN_WIDTH = N_ATTN_HEADS * ATTN_HEAD_DIM
ATTN_RADIUS = 64
ROPE_THETA = 500000.0
ROPE_DIM = ATTN_HEAD_DIM // 4
N_RET_HEADS = 4
RET_HEAD_DIM = 128
RET_WIDTH = N_RET_HEADS * RET_HEAD_DIM
RET_THETA = 10000.0
RET_CHUNK = 128
IN_COLS = 3 * ATTN_WIDTH + 4 * RET_WIDTH
N_EXPERTS = 8
D_FF = 2816
NORM_EPS = 1e-6
NEG_INF = -1e30
TINY = 1e-30

LANES = 128
N_COL_BLOCKS = IN_COLS // LANES
VMEM_LIMIT = 56 * 1024 * 1024

TM = 512
TME = 512
FF_CHUNKS = ((0, 1024), (1024, 2048), (2048, 2816))
ATT_BQ = 128
ATT_W = ATT_BQ + 2 * ATTN_RADIUS
ATT_NB = 8
RET_UNROLL = 8
DMA_GROUP = 8


def _cparams(sem):
    return pltpu.CompilerParams(dimension_semantics=sem, vmem_limit_bytes=VMEM_LIMIT)


def _rms(x, g):
    return x * lax.rsqrt(jnp.mean(x * x, axis=-1, keepdims=True) + NORM_EPS) * g


def _silu(x):
    return x / (1.0 + jnp.exp(-x))


def _in_proj_body(x_ref, g_ref, w_ref, ca_ref, sa1_ref, sa2_ref, cr_ref, sr_ref, o_ref):
    h = _rms(x_ref[...], g_ref[...]).astype(BF16)
    for c in range(7):
        acc = jnp.dot(h, w_ref[:, c * 512:(c + 1) * 512], preferred_element_type=F32)
        for s in range(4):
            a = acc[:, s * LANES:(s + 1) * LANES]
            if c in (0, 1):
                a = (a * ca_ref[...] + pltpu.roll(a, ROPE_DIM // 2, 1) * sa1_ref[...]
                     + pltpu.roll(a, LANES - ROPE_DIM // 2, 1) * sa2_ref[...])
                if c == 0:
                    a = a * (ATTN_HEAD_DIM ** -0.5)
            elif c in (3, 4):
                a = a * cr_ref[...] + pltpu.roll(a, RET_HEAD_DIM // 2, 1) * sr_ref[...]
                if c == 4:
                    a = a * (RET_HEAD_DIM ** -0.5)
            o_ref[c * 4 + s] = a.astype(BF16)


def _in_proj(x2d, g, w, tabs, seq):
    t = x2d.shape[0]
    nblk = seq // TM
    tab_spec = pl.BlockSpec((TM, LANES), lambda i: (i % nblk, 0))
    return pl.pallas_call(
        _in_proj_body,
        grid=(t // TM,),
        in_specs=[pl.BlockSpec((TM, D_MODEL), lambda i: (i, 0)),
                  pl.BlockSpec((1, D_MODEL), lambda i: (0, 0)),
                  pl.BlockSpec((D_MODEL, IN_COLS), lambda i: (0, 0)),
                  tab_spec, tab_spec, tab_spec, tab_spec, tab_spec],
        out_specs=pl.BlockSpec((N_COL_BLOCKS, TM, LANES), lambda i: (0, i, 0)),
        out_shape=jax.ShapeDtypeStruct((N_COL_BLOCKS, t, LANES), BF16),
        compiler_params=_cparams(("arbitrary",)),
        name="in_proj",
    )(x2d, g, w, *tabs)


def _attn_body(q_ref, k_ref, v_ref, o_ref,
               k1, v1, q4, k4, v4, q16, k16, v16, bias,
               acc_a, ms_a, ls_a, acc_b, ms_b, ls_b, *, seq):
    st_a = (acc_a, ms_a, ls_a)
    st_b = (acc_b, ms_b, ls_b)
    stage = 512
    pad = ATTN_RADIUS
    lane = lax.broadcasted_iota(jnp.int32, (ATT_BQ, LANES), 1)
    head0 = lane < ATTN_HEAD_DIM
    khead0 = lax.broadcasted_iota(jnp.int32, (ATT_W, LANES), 1) < ATTN_HEAD_DIM

    qi = lax.broadcasted_iota(jnp.int32, (ATT_BQ, ATT_W), 0)
    kj = lax.broadcasted_iota(jnp.int32, (ATT_BQ, ATT_W), 1)
    band = jnp.abs(kj - pad - qi) <= ATTN_RADIUS
    for e in range(4):
        ok = band
        if e & 1:
            ok = ok & (kj >= pad)
        if e & 2:
            ok = ok & (kj < ATT_BQ + pad)
        bias[e] = jnp.where(ok, 0.0, NEG_INF)

    for arr in (k1, v1, k4, v4, k16, v16):
        arr[pl.ds(0, pad), :] = jnp.zeros((pad, LANES), BF16)
        arr[pl.ds(seq + pad, pad), :] = jnp.zeros((pad, LANES), BF16)

    cls4, cls16 = seq // 4, seq // 16
    xf, x4f = st_a[0], st_b[0]
    for src, d1, d4, d16 in ((q_ref, None, q4, q16), (k_ref, k1, k4, k16), (v_ref, v1, v4, v16)):
        off = 0 if d1 is None else pad

        def upcast(i, c, src=src, d1=d1):
            r0 = pl.multiple_of(i * stage, stage)
            x = src[pl.ds(r0, stage), :]
            xf[pl.ds(r0, stage), :] = x.astype(F32)
            if d1 is not None:
                d1[pl.ds(pad + r0, stage), :] = x
            return c
        lax.fori_loop(0, seq // stage, upcast, 0)
        ch = min(cls4, 256)
        for a in range(4):
            def to4(i, c, d4=d4, a=a, ch=ch, off=off):
                j0 = pl.multiple_of(i * ch, ch)
                x = xf[pl.ds(a + 4 * j0, ch, stride=4), :]
                x4f[pl.ds(a * cls4 + j0, ch), :] = x
                d4[pl.ds(off + a * cls4 + j0, ch), :] = x.astype(BF16)
                return c
            lax.fori_loop(0, cls4 // ch, to4, 0)
        ch = min(cls16, 256)
        for a in range(4):
            for b in range(4):
                def to16(i, c, d16=d16, a=a, b=b, ch=ch, off=off):
                    i0 = pl.multiple_of(i * ch, ch)
                    d16[pl.ds(off + a * cls4 + b * cls16 + i0, ch), :] = (
                        x4f[pl.ds(a * cls4 + b + 4 * i0, ch, stride=4), :].astype(BF16))
                    return c
                lax.fori_loop(0, cls16 // ch, to16, 0)

    def relayout_16_to_4(src, dst):
        ch = min(cls16, ATT_BQ)
        for a in range(4):
            for b in range(4):
                def body(i, c, a=a, b=b):
                    i0 = pl.multiple_of(i * ch, ch)
                    for s_, d_ in zip(src, dst):
                        d_[pl.ds(a * cls4 + b + 4 * i0, ch, stride=4), :] = (
                            s_[pl.ds(a * cls4 + b * cls16 + i0, ch), :])
                    return c
                lax.fori_loop(0, cls16 // ch, body, 0)

    def relayout_4_to_1(src, dst):
        ch = ATT_BQ
        for a in range(4):
            def body(i, c, a=a):
                j0 = pl.multiple_of(i * ch, ch)
                for s_, d_ in zip(src, dst):
                    d_[pl.ds(a + 4 * j0, ch, stride=4), :] = s_[pl.ds(a * cls4 + j0, ch), :]
                return c
            lax.fori_loop(0, cls4 // ch, body, 0)

    def branch(d, qd, kd, vd, state, mode):
        acc, ms, ls = state
        nb = ATT_NB
        nblk = seq // d // ATT_BQ

        def group(g, c):
            blocks = []
            for u in range(nb):
                t = g * nb + u
                base = pl.multiple_of(t * ATT_BQ, ATT_BQ)
                n = t & (nblk - 1)
                edge = jnp.where(n == 0, 1, 0) + jnp.where(n == nblk - 1, 2, 0)
                blocks.append((base, edge))
            scores = []
            for base, edge in blocks:
                qb = qd[pl.ds(base, ATT_BQ), :]
                kw = kd[pl.ds(base, ATT_W), :]
                for hd in range(2):
                    qh = jnp.where(head0 if hd == 0 else ~head0, qb, jnp.zeros_like(qb))
                    scores.append(lax.dot_general(qh, kw, (((1,), (1,)), ((), ())),
                                                  preferred_element_type=F32) + bias[edge])
            maxes = [jnp.max(s, axis=1, keepdims=True) for s in scores]
            probs = [jnp.exp(s - m).astype(BF16) for s, m in zip(scores, maxes)]
            fulls = []
            for i, (base, edge) in enumerate(blocks):
                vw = vd[pl.ds(base, ATT_W), :]
                for hd in range(2):
                    vh = jnp.where(khead0 if hd == 0 else ~khead0, vw, jnp.ones_like(vw))
                    fulls.append(jnp.dot(probs[2 * i + hd], vh, preferred_element_type=F32))
            for i, (base, edge) in enumerate(blocks):
                f0, f1 = fulls[2 * i], fulls[2 * i + 1]
                o_new = jnp.where(head0, f0, f1)
                l_new = pltpu.roll(jnp.where(head0, f1, f0), ATTN_HEAD_DIM, 1)
                l_new = jnp.maximum(l_new, TINY)
                m_new = jnp.where(head0, maxes[2 * i], maxes[2 * i + 1])
                rows = pl.ds(base, ATT_BQ)
                if mode == "first":
                    acc[rows, :] = o_new
                    ms[rows, :] = m_new
                    ls[rows, :] = l_new
                    continue
                m_old = ms[rows, :]
                m_tot = jnp.maximum(m_old, m_new)
                a_old = jnp.exp(m_old - m_tot)
                a_new = jnp.exp(m_new - m_tot)
                o_tot = acc[rows, :] * a_old + o_new * a_new
                l_tot = ls[rows, :] * a_old + l_new * a_new
                if mode == "merge":
                    acc[rows, :] = o_tot
                    ls[rows, :] = l_tot
                    ms[rows, :] = m_tot
                else:
                    o_ref[rows, :] = (o_tot / l_tot).astype(BF16)
            return c

        lax.fori_loop(0, seq // (ATT_BQ * nb), group, 0)

    branch(16, q16, k16, v16, st_a, "first")
    relayout_16_to_4(st_a, st_b)
    branch(4, q4, k4, v4, st_b, "merge")
    relayout_4_to_1(st_b, st_a)
    branch(1, q_ref, k1, v1, st_a, "last")


def _attention(proj, batch, seq):
    t = batch * seq
    padded = seq + 2 * ATTN_RADIUS
    qkv_spec = lambda off: pl.BlockSpec((None, seq, LANES), lambda b, hp: (off + hp, b, 0),
                                        pipeline_mode=pl.Buffered(1))
    return pl.pallas_call(
        functools.partial(_attn_body, seq=seq),
        grid=(batch, ATTN_WIDTH // LANES),
        in_specs=[qkv_spec(0), qkv_spec(4), qkv_spec(8)],
        out_specs=pl.BlockSpec((seq, LANES), lambda b, hp: (b, hp)),
        out_shape=jax.ShapeDtypeStruct((t, ATTN_WIDTH), BF16),
        scratch_shapes=[pltpu.VMEM((padded, LANES), BF16), pltpu.VMEM((padded, LANES), BF16),
                        pltpu.VMEM((seq, LANES), BF16),
                        pltpu.VMEM((padded, LANES), BF16), pltpu.VMEM((padded, LANES), BF16),
                        pltpu.VMEM((seq, LANES), BF16),
                        pltpu.VMEM((padded, LANES), BF16), pltpu.VMEM((padded, LANES), BF16),
                        pltpu.VMEM((4, ATT_BQ, ATT_W), F32)]
        + [pltpu.VMEM((seq, LANES), F32)] * 6,
        compiler_params=_cparams(("arbitrary", "arbitrary")),
        name="dilated_attn",
    )(proj, proj, proj)


def _ret_body(lg_ref, q_ref, k_ref, v_ref, gate_ref, gain_ref, o_ref, ob, dec, *, seq):
    c_len = RET_CHUNK
    n_chunks = seq // c_len
    hd = pl.program_id(1)
    lf = lg_ref[0, hd]
    lb = lg_ref[1, hd]
    ri = lax.broadcasted_iota(jnp.int32, (c_len, c_len), 0).astype(F32)
    ci = lax.broadcasted_iota(jnp.int32, (c_len, c_len), 1).astype(F32)
    diff = ri - ci
    dec[0] = jnp.where(diff >= 0, jnp.exp(lf * jnp.maximum(diff, 0.0)),
                       jnp.exp(lb * jnp.maximum(-diff, 0.0)))
    dec[1] = jnp.exp(lf * (ri + 1.0))
    dec[2] = jnp.exp(lf * (c_len - 1.0 - ri))
    dec[3] = jnp.exp(lb * (c_len - ri))
    dec[4] = jnp.exp(lb * ri)
    cdf = jnp.exp(lf * c_len)
    cdb = jnp.exp(lb * c_len)
    tdot = (((0,), (0,)), ((), ()))

    def bwd(jg, state):
        for u in range(RET_UNROLL):
            j = jg * RET_UNROLL + u
            rows = pl.ds(pl.multiple_of((n_chunks - 1 - j) * c_len, c_len), c_len)
            qn = q_ref[rows, :].astype(F32)
            kn = k_ref[rows, :].astype(F32)
            ob[rows, :] = jnp.dot((qn * dec[3]).astype(BF16), state.astype(BF16),
                                  preferred_element_type=F32)
            kv = lax.dot_general((kn * dec[4]).astype(BF16), v_ref[rows, :], tdot,
                                 preferred_element_type=F32)
            state = state * cdb + kv
        return state

    lax.fori_loop(0, n_chunks // RET_UNROLL, bwd, jnp.zeros((c_len, c_len), F32))

    gain = gain_ref[...]

    def fwd(jg, state):
        row_sets = [pl.ds(pl.multiple_of((jg * RET_UNROLL + u) * c_len, c_len), c_len)
                    for u in range(RET_UNROLL)]
        states = []
        for rows in row_sets:
            states.append(state)
            kn = k_ref[rows, :].astype(F32)
            kv = lax.dot_general((kn * dec[2]).astype(BF16), v_ref[rows, :], tdot,
                                 preferred_element_type=F32)
            state = state * cdf + kv
        for rows, st in zip(row_sets, states):
            qb = q_ref[rows, :]
            vb = v_ref[rows, :]
            sc = lax.dot_general(qb, k_ref[rows, :], (((1,), (1,)), ((), ())),
                                 preferred_element_type=F32) * dec[0]
            tot = (ob[rows, :] + jnp.dot(sc.astype(BF16), vb, preferred_element_type=F32)
                   + jnp.dot((qb.astype(F32) * dec[1]).astype(BF16), st.astype(BF16),
                             preferred_element_type=F32))
            y = tot - jnp.mean(tot, axis=-1, keepdims=True)
            y = y * lax.rsqrt(jnp.mean(y * y, axis=-1, keepdims=True) + NORM_EPS) * gain
            o_ref[rows, :] = (y * _silu(gate_ref[rows, :].astype(F32))).astype(BF16)
        return state

    lax.fori_loop(0, n_chunks // RET_UNROLL, fwd, jnp.zeros((c_len, c_len), F32))


def _retention(proj, gain, lg, batch, seq):
    t = batch * seq
    spec = lambda off: pl.BlockSpec((None, seq, LANES), lambda b, hd: (off + hd, b, 0))
    return pl.pallas_call(
        functools.partial(_ret_body, seq=seq),
        grid=(batch, N_RET_HEADS),
        in_specs=[pl.BlockSpec(memory_space=pltpu.SMEM),
                  spec(12), spec(16), spec(20), spec(24),
                  pl.BlockSpec((None, 1, LANES), lambda b, hd: (hd, 0, 0))],
        out_specs=pl.BlockSpec((seq, LANES), lambda b, hd: (b, hd)),
        out_shape=jax.ShapeDtypeStruct((t, RET_WIDTH), BF16),
        scratch_shapes=[pltpu.VMEM((seq, LANES), F32),
                        pltpu.VMEM((5, RET_CHUNK, RET_CHUNK), F32)],
        compiler_params=_cparams(("arbitrary", "arbitrary")),
        name="retention",
    )(lg, proj, proj, proj, proj, gain)


def _out_proj(x_ref, a_ref, r_ref, wo_ref, ag_ref, hm_ref):
    a = a_ref[...].astype(F32)
    mean = jnp.dot((a * a).astype(BF16), hm_ref[...], preferred_element_type=F32)
    an = (a * lax.rsqrt(mean + NORM_EPS) * ag_ref[...]).astype(BF16)
    return (x_ref[...]
            + jnp.dot(an, wo_ref[0:ATTN_WIDTH, :], preferred_element_type=F32)
            + jnp.dot(r_ref[...], wo_ref[ATTN_WIDTH:, :], preferred_element_type=F32))


def _post_dense_body(x_ref, a_ref, r_ref, wo_ref, ag_ref, hm_ref, g_ref, w1_ref, w3_ref, w2_ref,
                     fg_ref, o_ref, *, final_norm):
    x2 = _out_proj(x_ref, a_ref, r_ref, wo_ref, ag_ref, hm_ref)
    h = _rms(x2, g_ref[...]).astype(BF16)
    ffn = None
    for c0, c1 in FF_CHUNKS:
        a = jnp.dot(h, w1_ref[:, c0:c1], preferred_element_type=F32)
        b = jnp.dot(h, w3_ref[:, c0:c1], preferred_element_type=F32)
        u = (_silu(a) * b).astype(BF16)
        d = jnp.dot(u, w2_ref[c0:c1, :], preferred_element_type=F32)
        ffn = d if ffn is None else ffn + d
    y = x2 + ffn
    if final_norm:
        y = _rms(y, fg_ref[...])
    o_ref[...] = y


def _post_dense(x2d, att, ret, wo, ag, hm, g, w1, w3, w2, fg, final_norm):
    t = x2d.shape[0]
    full = lambda a: pl.BlockSpec(a.shape, lambda i: (0,) * a.ndim)
    return pl.pallas_call(
        functools.partial(_post_dense_body, final_norm=final_norm),
        grid=(t // TM,),
        in_specs=[pl.BlockSpec((TM, D_MODEL), lambda i: (i, 0)),
                  pl.BlockSpec((TM, ATTN_WIDTH), lambda i: (i, 0)),
                  pl.BlockSpec((TM, RET_WIDTH), lambda i: (i, 0)),
                  full(wo), full(ag), full(hm), full(g), full(w1), full(w3), full(w2), full(fg)],
        out_specs=pl.BlockSpec((TM, D_MODEL), lambda i: (i, 0)),
        out_shape=jax.ShapeDtypeStruct((t, D_MODEL), F32),
        compiler_params=_cparams(("arbitrary",)),
        name="post_dense",
    )(x2d, att, ret, wo, ag, hm, g, w1, w3, w2, fg)


def _post_route_body(x_ref, a_ref, r_ref, wo_ref, ag_ref, hm_ref, g_ref, rt_ref,
                     x2_ref, h_ref, ri_ref, rf_ref, cnt_ref, base):
    @pl.when(pl.program_id(0) == 0)
    def _():
        base[...] = jnp.zeros_like(base)

    x2 = _out_proj(x_ref, a_ref, r_ref, wo_ref, ag_ref, hm_ref)
    x2_ref[...] = x2
    h = _rms(x2, g_ref[...])
    h_ref[...] = h
    hi = h.astype(BF16)
    lo = (h - hi.astype(F32)).astype(BF16)
    parts = (jnp.dot(hi, rt_ref[...], preferred_element_type=F32)
             + jnp.dot(lo, rt_ref[...], preferred_element_type=F32))
    logits = parts + pltpu.roll(parts, LANES - N_EXPERTS, 1)
    lane = lax.broadcasted_iota(jnp.int32, (TM, LANES), 1).astype(F32)
    lg = jnp.where(lane < N_EXPERTS, logits, -jnp.inf)
    m1 = jnp.max(lg, axis=1, keepdims=True)
    i1 = jnp.min(jnp.where(lg == m1, lane, float(LANES)), axis=1, keepdims=True)
    lg2 = jnp.where(lane == i1, -jnp.inf, lg)
    m2 = jnp.max(lg2, axis=1, keepdims=True)
    i2 = jnp.min(jnp.where(lg2 == m2, lane, float(LANES)), axis=1, keepdims=True)
    e = jnp.exp(m2 - m1)
    g1 = 1.0 / (1.0 + e)
    g2 = e / (1.0 + e)
    sel1 = lane == i1
    sel2 = lane == i2
    onehot = jnp.where(sel1 | sel2, 1.0, 0.0)
    row = lax.broadcasted_iota(jnp.int32, (TM, TM), 0)
    col = lax.broadcasted_iota(jnp.int32, (TM, TM), 1)
    earlier = jnp.where(col < row, 1.0, 0.0).astype(BF16)
    cnt = jnp.dot(earlier, onehot.astype(BF16), preferred_element_type=F32) + base[...]
    rank1 = jnp.sum(jnp.where(sel1, cnt, 0.0), axis=1, keepdims=True)
    rank2 = jnp.sum(jnp.where(sel2, cnt, 0.0), axis=1, keepdims=True)
    new_base = base[...] + jnp.sum(onehot, axis=0, keepdims=True)
    base[...] = new_base
    cnt_ref[...] = jnp.broadcast_to(new_base, cnt_ref.shape)
    ri = jnp.where(lane == 0, i1, jnp.where(lane == 1, i2,
                   jnp.where(lane == 2, rank1, jnp.where(lane == 3, rank2, 0.0))))
    ri_ref[...] = ri.astype(jnp.int32)
    rf_ref[...] = jnp.where(lane == 0, g1, jnp.where(lane == 1, g2, 0.0))


def _post_route(x2d, att, ret, wo, ag, hm, g, router_pad):
    t = x2d.shape[0]
    full = lambda a: pl.BlockSpec(a.shape, lambda i: (0,) * a.ndim)
    row = lambda w: pl.BlockSpec((TM, w), lambda i: (i, 0))
    return pl.pallas_call(
        _post_route_body,
        grid=(t // TM,),
        in_specs=[row(D_MODEL), row(ATTN_WIDTH), row(RET_WIDTH),
                  full(wo), full(ag), full(hm), full(g), full(router_pad)],
        out_specs=[row(D_MODEL), row(D_MODEL), row(LANES), row(LANES),
                   pl.BlockSpec((8, LANES), lambda i: (0, 0))],
        out_shape=[jax.ShapeDtypeStruct((t, D_MODEL), F32),
                   jax.ShapeDtypeStruct((t, D_MODEL), F32),
                   jax.ShapeDtypeStruct((t, LANES), jnp.int32),
                   jax.ShapeDtypeStruct((t, LANES), F32),
                   jax.ShapeDtypeStruct((8, LANES), F32)],
        scratch_shapes=[pltpu.VMEM((1, LANES), F32)],
        compiler_params=_cparams(("arbitrary",)),
        name="post_route",
    )(x2d, att, ret, wo, ag, hm, g, router_pad)


def _step_indices(pos_hbm, idx, sem_i):
    i = pl.program_id(0)
    slot = i % 2
    width = 2 * TM

    def fetch(step, s):
        dst = idx.at[pl.ds(pl.multiple_of(s * width, width), width)]
        return pltpu.make_async_copy(pos_hbm.at[step], dst, sem_i.at[s])

    @pl.when(i == 0)
    def _():
        fetch(0, 0).start()

    fetch(i, slot).wait()

    @pl.when(i + 1 < pl.num_programs(0))
    def _():
        fetch(i + 1, 1 - slot).start()

    return pl.multiple_of(slot * width, width)


def _scatter_body(ginfo, pos_hbm, h_ref, o_hbm, idx, zbuf, sem_i, sem_r, sem_z, *, n_sorted):
    i = pl.program_id(0)

    @pl.when(i == 0)
    def _():
        zbuf[...] = jnp.zeros_like(zbuf)
        total = ginfo[N_EXPERTS - 1]
        fills = []
        for e in range(N_EXPERTS):
            start = pl.multiple_of(jnp.maximum(ginfo[e] - TME, 0), TME)
            fills.append((ginfo[N_EXPERTS + e] > 0, start))
        for j in range(N_EXPERTS):
            start = n_sorted - (j + 1) * TME
            fills.append((start >= total, start))
        for phase in ("start", "wait"):
            for cond, start in fills:
                @pl.when(cond)
                def _(start=start, phase=phase):
                    cp = pltpu.make_async_copy(zbuf, o_hbm.at[pl.ds(start, TME), :], sem_z)
                    cp.start() if phase == "start" else cp.wait()

    slot = _step_indices(pos_hbm, idx, sem_i)

    def copies(g, u):
        t = g * DMA_GROUP + u
        src = h_ref.at[g, pl.ds(u, 1), :]
        return (pltpu.make_async_copy(src, o_hbm.at[pl.ds(idx[slot + t], 1), :], sem_r),
                pltpu.make_async_copy(src, o_hbm.at[pl.ds(idx[slot + TM + t], 1), :], sem_r))

    def issue(g, c):
        for u in range(DMA_GROUP):
            for cp in copies(g, u):
                cp.start()
        return c

    lax.fori_loop(0, TM // DMA_GROUP, issue, 0)

    def drain(g, c):
        for u in range(DMA_GROUP):
            for cp in copies(g, u):
                cp.wait()
        return c

    lax.fori_loop(0, TM // DMA_GROUP, drain, 0)


def _scatter(ginfo, pos, h, n_sorted):
    t = h.shape[0]
    return pl.pallas_call(
        functools.partial(_scatter_body, n_sorted=n_sorted),
        grid=(t // TM,),
        in_specs=[pl.BlockSpec(memory_space=pltpu.SMEM),
                  pl.BlockSpec(memory_space=pl.ANY),
                  pl.BlockSpec((TM // DMA_GROUP, DMA_GROUP, D_MODEL), lambda i: (i, 0, 0))],
        out_specs=pl.BlockSpec(memory_space=pl.ANY),
        out_shape=jax.ShapeDtypeStruct((n_sorted, D_MODEL), F32),
        scratch_shapes=[pltpu.SMEM((4 * TM,), jnp.int32),
                        pltpu.VMEM((TME, D_MODEL), F32),
                        pltpu.SemaphoreType.DMA((2,)), pltpu.SemaphoreType.DMA,
                        pltpu.SemaphoreType.DMA],
        compiler_params=_cparams(("arbitrary",)),
        name="moe_dispatch",
    )(ginfo, pos, h.reshape(t // DMA_GROUP, DMA_GROUP, D_MODEL))


def _expert_body(te_ref, nv_ref, h_ref, w1_ref, w3_ref, w2_ref, o_ref):
    nv = nv_ref[pl.program_id(0)]

    @pl.when(nv == 0)
    def _():
        o_ref[...] = jnp.zeros_like(o_ref)

    @pl.when(nv > 0)
    def _():
        rows = lax.broadcasted_iota(jnp.int32, (TME, D_MODEL), 0)
        h = jnp.where(rows < nv, h_ref[...], 0.0).astype(BF16)
        ffn = None
        for c0, c1 in FF_CHUNKS:
            a = jnp.dot(h, w1_ref[:, c0:c1], preferred_element_type=F32)
            b = jnp.dot(h, w3_ref[:, c0:c1], preferred_element_type=F32)
            u = (_silu(a) * b).astype(BF16)
            d = jnp.dot(u, w2_ref[c0:c1, :], preferred_element_type=F32)
            ffn = d if ffn is None else ffn + d
        o_ref[...] = ffn


def _experts(tile_e, tile_nv, hs, w1, w3, w2):
    n_sorted = hs.shape[0]
    grid_spec = pltpu.PrefetchScalarGridSpec(
        num_scalar_prefetch=2,
        grid=(n_sorted // TME,),
        in_specs=[pl.BlockSpec((TME, D_MODEL), lambda j, te, nv: (j, 0)),
                  pl.BlockSpec((None, D_MODEL, D_FF), lambda j, te, nv: (te[j], 0, 0)),
                  pl.BlockSpec((None, D_MODEL, D_FF), lambda j, te, nv: (te[j], 0, 0)),
                  pl.BlockSpec((None, D_FF, D_MODEL), lambda j, te, nv: (te[j], 0, 0))],
        out_specs=pl.BlockSpec((TME, D_MODEL), lambda j, te, nv: (j, 0)),
    )
    return pl.pallas_call(
        _expert_body,
        grid_spec=grid_spec,
        out_shape=jax.ShapeDtypeStruct((n_sorted, D_MODEL), F32),
        compiler_params=_cparams(("arbitrary",)),
        name="moe_experts",
    )(tile_e, tile_nv, hs, w1, w3, w2)


def _combine_body(pos_hbm, x_ref, rf_ref, fg_ref, y_hbm, o_ref, idx, b1, b2, sem_i, sem_r,
                  *, final_norm):
    slot = _step_indices(pos_hbm, idx, sem_i)

    def copies(g, u):
        t = g * DMA_GROUP + u
        return (pltpu.make_async_copy(y_hbm.at[pl.ds(idx[slot + t], 1), :],
                                      b1.at[g, pl.ds(u, 1), :], sem_r),
                pltpu.make_async_copy(y_hbm.at[pl.ds(idx[slot + TM + t], 1), :],
                                      b2.at[g, pl.ds(u, 1), :], sem_r))

    def issue(g, c):
        for u in range(DMA_GROUP):
            for cp in copies(g, u):
                cp.start()
        return c

    lax.fori_loop(0, TM // DMA_GROUP, issue, 0)

    def drain(g, c):
        for u in range(DMA_GROUP):
            for cp in copies(g, u):
                cp.wait()
        return c

    lax.fori_loop(0, TM // DMA_GROUP, drain, 0)

    g1 = rf_ref[:, 0:1]
    g2 = rf_ref[:, 1:2]
    y = x_ref[...] + (g1 * b1[...].reshape(TM, D_MODEL) + g2 * b2[...].reshape(TM, D_MODEL))
    if final_norm:
        y = _rms(y, fg_ref[...])
    o_ref[...] = y


def _combine(pos, x2, rf, fg, ys, final_norm):
    t = x2.shape[0]
    return pl.pallas_call(
        functools.partial(_combine_body, final_norm=final_norm),
        grid=(t // TM,),
        in_specs=[pl.BlockSpec(memory_space=pl.ANY),
                  pl.BlockSpec((TM, D_MODEL), lambda i: (i, 0)),
                  pl.BlockSpec((TM, LANES), lambda i: (i, 0)),
                  pl.BlockSpec((1, D_MODEL), lambda i: (0, 0)),
                  pl.BlockSpec(memory_space=pl.ANY)],
        out_specs=pl.BlockSpec((TM, D_MODEL), lambda i: (i, 0)),
        out_shape=jax.ShapeDtypeStruct((t, D_MODEL), F32),
        scratch_shapes=[pltpu.SMEM((4 * TM,), jnp.int32),
                        pltpu.VMEM((TM // DMA_GROUP, DMA_GROUP, D_MODEL), F32),
                        pltpu.VMEM((TM // DMA_GROUP, DMA_GROUP, D_MODEL), F32),
                        pltpu.SemaphoreType.DMA((2,)), pltpu.SemaphoreType.DMA],
        compiler_params=_cparams(("arbitrary",)),
        name="moe_combine",
    )(pos, x2, rf, fg, ys)


def _moe(x2d, att, ret, wo, ag, hm, g, router_pad, w1, w3, w2, fg, final_norm):
    t = x2d.shape[0]
    n_sorted = 2 * t + N_EXPERTS * TME
    n_tiles = n_sorted // TME
    x2, h, ri, rf, cnt = _post_route(x2d, att, ret, wo, ag, hm, g, router_pad)

    counts = cnt[0, :N_EXPERTS].astype(jnp.int32)
    padded = ((counts + TME - 1) // TME) * TME
    ends = jnp.cumsum(padded)
    starts = ends - padded
    experts = jnp.arange(N_EXPERTS, dtype=jnp.int32)
    start_of = lambda e: jnp.sum(jnp.where(e[:, None] == experts[None, :], starts[None, :], 0), axis=1)
    pos1 = start_of(ri[:, 0]) + ri[:, 2]
    pos2 = start_of(ri[:, 1]) + ri[:, 3]
    pos = jnp.concatenate([pos1.reshape(t // TM, TM), pos2.reshape(t // TM, TM)], axis=1)
    tile_start = jnp.arange(n_tiles, dtype=jnp.int32) * TME
    tile_e = jnp.minimum(jnp.sum(tile_start[:, None] >= ends[None, :], axis=1), N_EXPERTS - 1)
    tile_e = tile_e.astype(jnp.int32)
    tile_nv = jnp.clip(starts[tile_e] + counts[tile_e] - tile_start, 0, TME)
    tile_nv = jnp.where(tile_start < ends[-1], tile_nv, 0).astype(jnp.int32)
    ginfo = jnp.concatenate([ends, padded]).astype(jnp.int32)

    hs = _scatter(ginfo, pos, h, n_sorted)
    ys = _experts(tile_e, tile_nv, hs, w1, w3, w2)
    return _combine(pos, x2, rf, fg, ys, final_norm)


def _rope_tables(seq):
    pos = jnp.arange(seq, dtype=F32)[:, None]
    lane = jnp.arange(LANES)
    j = lane % ATTN_HEAD_DIM
    half = ROPE_DIM // 2
    inv_a = ROPE_THETA ** (-jnp.arange(half, dtype=F32) * 2.0 / ROPE_DIM)
    ang = pos * inv_a[None, :]
    cos_a, sin_a = jnp.cos(ang), jnp.sin(ang)
    jc = jnp.where(j < half, j, j - half) % half
    ca = jnp.where(j[None, :] < ROPE_DIM, cos_a[:, jc], 1.0)
    sa1 = jnp.where(((j >= half) & (j < ROPE_DIM))[None, :], sin_a[:, jc], 0.0)
    sa2 = jnp.where((j < half)[None, :], -sin_a[:, jc], 0.0)
    half_r = RET_HEAD_DIM // 2
    inv_r = RET_THETA ** (-jnp.arange(half_r, dtype=F32) * 2.0 / RET_HEAD_DIM)
    ang_r = pos * inv_r[None, :]
    cos_r, sin_r = jnp.cos(ang_r), jnp.sin(ang_r)
    cr = cos_r[:, lane % half_r]
    sr = jnp.where((lane < half_r)[None, :], -sin_r[:, lane % half_r], sin_r[:, lane % half_r])
    return ca, sa1, sa2, cr, sr


def _pack_router(router):
    hi = router.astype(BF16)
    lo = (router - hi.astype(F32)).astype(BF16)
    packed = jnp.concatenate([hi, lo], axis=-1)
    return jnp.pad(packed, ((0, 0), (0, 0), (0, LANES - 2 * N_EXPERTS)))


def _trunk(x, w):
    batch, seq, _ = x.shape
    t = batch * seq
    assert seq % (16 * ATT_BQ) == 0 and seq % TM == 0 and t % TM == 0
    tabs = _rope_tables(seq)
    x2d = x.reshape(t, D_MODEL)
    depth = w["w_in"].shape[0]
    head_of = jnp.arange(ATTN_WIDTH) // ATTN_HEAD_DIM
    head_mean = jnp.where(head_of[:, None] == head_of[None, :], 1.0 / ATTN_HEAD_DIM, 0.0).astype(BF16)
    for layer in range(depth):
        last = layer == depth - 1
        ag = w["attn_out_g"][layer][None, :]
        proj = _in_proj(x2d, w["attn_norm_g"][layer][None, :], w["w_in"][layer], tabs, seq)
        att = _attention(proj, batch, seq)
        ret = _retention(proj, w["ret_out_g"][layer].reshape(N_RET_HEADS, 1, LANES),
                         w["lg"][layer], batch, seq)
        i = layer // 2
        if layer % 2 == 0:
            x2d = _post_dense(x2d, att, ret, w["w_out"][layer], ag, head_mean,
                              w["ffn_norm_g"][layer][None, :],
                              w["dense_w1"][i], w["dense_w3"][i], w["dense_w2"][i],
                              w["final_norm_g"], last)
        else:
            x2d = _moe(x2d, att, ret, w["w_out"][layer], ag, head_mean,
                       w["ffn_norm_g"][layer][None, :],
                       w["router_pad"][i], w["moe_w1"][i], w["moe_w3"][i], w["moe_w2"][i],
                       w["final_norm_g"], last)
    return x2d.reshape(batch, seq, D_MODEL)


def kernel(x_prompt, x_sample, attn_norm_g, w_in, attn_out_g, ret_out_g, ret_decay_fwd, ret_decay_bwd, w_out, ffn_norm_g, dense_w1, dense_w3, dense_w2, moe_router, moe_w1, moe_w3, moe_w2, final_norm_g):
    lg_f = jnp.log1p(-jnp.exp(ret_decay_fwd.astype(F32)))
    lg_b = jnp.log1p(-jnp.exp(ret_decay_bwd.astype(F32)))
    w = {
        "attn_norm_g": attn_norm_g, "attn_out_g": attn_out_g, "ret_out_g": ret_out_g,
        "ffn_norm_g": ffn_norm_g, "final_norm_g": final_norm_g[None, :],
        "lg": jnp.stack([lg_f, lg_b], axis=1),
        "w_in": w_in.astype(BF16), "w_out": w_out.astype(BF16),
        "dense_w1": dense_w1.astype(BF16), "dense_w3": dense_w3.astype(BF16),
        "dense_w2": dense_w2.astype(BF16),
        "moe_w1": moe_w1.astype(BF16), "moe_w3": moe_w3.astype(BF16), "moe_w2": moe_w2.astype(BF16),
        "router_pad": _pack_router(moe_router.astype(F32)),
    }
    return _trunk(x_prompt, w), _trunk(x_sample, w)
```

```python
import functools

import jax
import jax.numpy as jnp
from jax import lax
from jax.experimental import pallas as pl
from jax.experimental.pallas import tpu as pltpu

F32 = jnp.float32
BF16 = jnp.bfloat16

D_MODEL = 1024
N_ATTN_HEADS = 8
ATTN_HEAD_DIM = 64
ATTN_WIDTH = N_ATTN_HEADS * ATTN_HEAD_DIM
ATTN_RADIUS = 64
ROPE_THETA = 500000.0
ROPE_DIM = ATTN_HEAD_DIM // 4
N_RET_HEADS = 4
RET_HEAD_DIM = 128
RET_WIDTH = N_RET_HEADS * RET_HEAD_DIM
RET_THETA = 10000.0
RET_CHUNK = 128
IN_COLS = 3 * ATTN_WIDTH + 4 * RET_WIDTH
N_EXPERTS = 8
D_FF = 2816
NORM_EPS = 1e-6
NEG_INF = -1e30
TINY = 1e-30

LANES = 128
N_COL_BLOCKS = IN_COLS // LANES
VMEM_LIMIT = 56 * 1024 * 1024
ATTN_VMEM_LIMIT = 58 * 1024 * 1024

TM = 512
TME = 512
FF_CHUNKS = ((0, 1024), (1024, 2048), (2048, 2816))
ATT_BQ = 128
ATT_W = ATT_BQ + 2 * ATTN_RADIUS
ROUTE_PARTS = 2
ATT_NB = 8
RET_UNROLL = 8
DMA_GROUP = 8


def _cparams(sem, vmem_limit=VMEM_LIMIT):
    return pltpu.CompilerParams(dimension_semantics=sem, vmem_limit_bytes=vmem_limit)


def _rms(x, g):
    return x * lax.rsqrt(jnp.mean(x * x, axis=-1, keepdims=True) + NORM_EPS) * g


def _silu(x):
    return x / (1.0 + jnp.exp(-x))


def _in_proj_body(x_ref, g_ref, w_ref, ca_ref, sa1_ref, sa2_ref, cr_ref, sr_ref, o_ref):
    h = _rms(x_ref[...], g_ref[...]).astype(BF16)
    for c in range(7):
        acc = jnp.dot(h, w_ref[:, c * 512:(c + 1) * 512], preferred_element_type=F32)
        for s in range(4):
            a = acc[:, s * LANES:(s + 1) * LANES]
            if c in (0, 1):
                a = (a * ca_ref[...] + pltpu.roll(a, ROPE_DIM // 2, 1) * sa1_ref[...]
                     + pltpu.roll(a, LANES - ROPE_DIM // 2, 1) * sa2_ref[...])
                if c == 0:
                    a = a * (ATTN_HEAD_DIM ** -0.5)
            elif c in (3, 4):
                a = a * cr_ref[...] + pltpu.roll(a, RET_HEAD_DIM // 2, 1) * sr_ref[...]
                if c == 4:
                    a = a * (RET_HEAD_DIM ** -0.5)
            o_ref[c * 4 + s] = a.astype(BF16)


def _in_proj(x2d, g, w, tabs, seq):
    t = x2d.shape[0]
    nblk = seq // TM
    tab_spec = pl.BlockSpec((TM, LANES), lambda i: (i % nblk, 0))
    return pl.pallas_call(
        _in_proj_body,
        grid=(t // TM,),
        in_specs=[pl.BlockSpec((TM, D_MODEL), lambda i: (i, 0)),
                  pl.BlockSpec((1, D_MODEL), lambda i: (0, 0)),
                  pl.BlockSpec((D_MODEL, IN_COLS), lambda i: (0, 0)),
                  tab_spec, tab_spec, tab_spec, tab_spec, tab_spec],
        out_specs=pl.BlockSpec((N_COL_BLOCKS, TM, LANES), lambda i: (0, i, 0)),
        out_shape=jax.ShapeDtypeStruct((N_COL_BLOCKS, t, LANES), BF16),
        compiler_params=_cparams(("arbitrary",)),
        name="in_proj",
    )(x2d, g, w, *tabs)


def _attn_body(q_ref, k_ref, v_ref, o_ref,
               k1, v1, q4, k4, v4, q16, k16, v16, bias,
               acc_a, ms_a, ls_a, acc_b, ms_b, ls_b, *, seq):
    st_a = (acc_a, ms_a, ls_a)
    st_b = (acc_b, ms_b, ls_b)
    stage = 512
    pad = ATTN_RADIUS
    lane = lax.broadcasted_iota(jnp.int32, (ATT_BQ, LANES), 1)
    head0 = lane < ATTN_HEAD_DIM
    khead0 = lax.broadcasted_iota(jnp.int32, (ATT_W, LANES), 1) < ATTN_HEAD_DIM

    qi = lax.broadcasted_iota(jnp.int32, (ATT_BQ, ATT_W), 0)
    kj = lax.broadcasted_iota(jnp.int32, (ATT_BQ, ATT_W), 1)
    band = jnp.abs(kj - pad - qi) <= ATTN_RADIUS
    for e in range(4):
        ok = band
        if e & 1:
            ok = ok & (kj >= pad)
        if e & 2:
            ok = ok & (kj < ATT_BQ + pad)
        bias[e] = jnp.where(ok, 0.0, NEG_INF)

    for arr in (k1, v1, k4, v4, k16, v16):
        arr[pl.ds(0, pad), :] = jnp.zeros((pad, LANES), BF16)
        arr[pl.ds(seq + pad, pad), :] = jnp.zeros((pad, LANES), BF16)

    cls4, cls16 = seq // 4, seq // 16
    xf, x4f = st_a[0], st_b[0]
    for src, d1, d4, d16 in ((q_ref, None, q4, q16), (k_ref, k1, k4, k16), (v_ref, v1, v4, v16)):
        off = 0 if d1 is None else pad

        def upcast(i, c, src=src, d1=d1):
            r0 = pl.multiple_of(i * stage, stage)
            x = src[pl.ds(r0, stage), :]
            xf[pl.ds(r0, stage), :] = x.astype(F32)
            if d1 is not None:
                d1[pl.ds(pad + r0, stage), :] = x
            return c
        lax.fori_loop(0, seq // stage, upcast, 0)
        ch = min(cls4, 256)
        for a in range(4):
            def to4(i, c, d4=d4, a=a, ch=ch, off=off):
                j0 = pl.multiple_of(i * ch, ch)
                x = xf[pl.ds(a + 4 * j0, ch, stride=4), :]
                x4f[pl.ds(a * cls4 + j0, ch), :] = x
                d4[pl.ds(off + a * cls4 + j0, ch), :] = x.astype(BF16)
                return c
            lax.fori_loop(0, cls4 // ch, to4, 0)
        ch = min(cls16, 256)
        for a in range(4):
            for b in range(4):
                def to16(i, c, d16=d16, a=a, b=b, ch=ch, off=off):
                    i0 = pl.multiple_of(i * ch, ch)
                    d16[pl.ds(off + a * cls4 + b * cls16 + i0, ch), :] = (
                        x4f[pl.ds(a * cls4 + b + 4 * i0, ch, stride=4), :].astype(BF16))
                    return c
                lax.fori_loop(0, cls16 // ch, to16, 0)

    def relayout_16_to_4(src, dst):
        ch = min(cls16, ATT_BQ)
        for a in range(4):
            for b in range(4):
                def body(i, c, a=a, b=b):
                    i0 = pl.multiple_of(i * ch, ch)
                    for s_, d_ in zip(src, dst):
                        d_[pl.ds(a * cls4 + b + 4 * i0, ch, stride=4), :] = (
                            s_[pl.ds(a * cls4 + b * cls16 + i0, ch), :])
                    return c
                lax.fori_loop(0, cls16 // ch, body, 0)

    def relayout_4_to_1(src, dst):
        ch = ATT_BQ
        for a in range(4):
            def body(i, c, a=a):
                j0 = pl.multiple_of(i * ch, ch)
                for s_, d_ in zip(src, dst):
                    d_[pl.ds(a + 4 * j0, ch, stride=4), :] = s_[pl.ds(a * cls4 + j0, ch), :]
                return c
            lax.fori_loop(0, cls4 // ch, body, 0)

    def branch(d, qd, kd, vd, state, mode):
        acc, ms, ls = state
        nb = ATT_NB
        nblk = seq // d // ATT_BQ

        def group(g, c):
            blocks = []
            for u in range(nb):
                t = g * nb + u
                base = pl.multiple_of(t * ATT_BQ, ATT_BQ)
                n = t & (nblk - 1)
                edge = jnp.where(n == 0, 1, 0) + jnp.where(n == nblk - 1, 2, 0)
                blocks.append((base, edge))
            scores = []
            for base, edge in blocks:
                qb = qd[pl.ds(base, ATT_BQ), :]
                kw = kd[pl.ds(base, ATT_W), :]
                for hd in range(2):
                    qh = jnp.where(head0 if hd == 0 else ~head0, qb, jnp.zeros_like(qb))
                    scores.append(lax.dot_general(qh, kw, (((1,), (1,)), ((), ())),
                                                  preferred_element_type=F32) + bias[edge])
            maxes = [jnp.max(s, axis=1, keepdims=True) for s in scores]
            probs = [jnp.exp(s - m).astype(BF16) for s, m in zip(scores, maxes)]
            fulls = []
            for i, (base, edge) in enumerate(blocks):
                vw = vd[pl.ds(base, ATT_W), :]
                for hd in range(2):
                    vh = jnp.where(khead0 if hd == 0 else ~khead0, vw, jnp.ones_like(vw))
                    fulls.append(jnp.dot(probs[2 * i + hd], vh, preferred_element_type=F32))
            for i, (base, edge) in enumerate(blocks):
                f0, f1 = fulls[2 * i], fulls[2 * i + 1]
                o_new = jnp.where(head0, f0, f1)
                l_new = pltpu.roll(jnp.where(head0, f1, f0), ATTN_HEAD_DIM, 1)
                l_new = jnp.maximum(l_new, TINY)
                m_new = jnp.where(head0, maxes[2 * i], maxes[2 * i + 1])
                rows = pl.ds(base, ATT_BQ)
                if mode == "first":
                    acc[rows, :] = o_new
                    ms[rows, :] = m_new
                    ls[rows, :] = l_new
                    continue
                m_old = ms[rows, :]
                m_tot = jnp.maximum(m_old, m_new)
                a_old = jnp.exp(m_old - m_tot)
                a_new = jnp.exp(m_new - m_tot)
                o_tot = acc[rows, :] * a_old + o_new * a_new
                l_tot = ls[rows, :] * a_old + l_new * a_new
                if mode == "merge":
                    acc[rows, :] = o_tot
                    ls[rows, :] = l_tot
                    ms[rows, :] = m_tot
                else:
                    o_ref[rows, :] = (o_tot / l_tot).astype(BF16)
            return c

        lax.fori_loop(0, seq // (ATT_BQ * nb), group, 0)

    branch(16, q16, k16, v16, st_a, "first")
    relayout_16_to_4(st_a, st_b)
    branch(4, q4, k4, v4, st_b, "merge")
    relayout_4_to_1(st_b, st_a)
    branch(1, q_ref, k1, v1, st_a, "last")


def _attention(proj, batch, seq):
    t = batch * seq
    padded = seq + 2 * ATTN_RADIUS
    qkv_spec = lambda off, mode=None: pl.BlockSpec(
        (None, seq, LANES), lambda b, hp: (off + hp, b, 0), pipeline_mode=mode)
    return pl.pallas_call(
        functools.partial(_attn_body, seq=seq),
        grid=(batch, ATTN_WIDTH // LANES),
        in_specs=[qkv_spec(0), qkv_spec(4), qkv_spec(8, pl.Buffered(1))],
        out_specs=pl.BlockSpec((seq, LANES), lambda b, hp: (b, hp)),
        out_shape=jax.ShapeDtypeStruct((t, ATTN_WIDTH), BF16),
        scratch_shapes=[pltpu.VMEM((padded, LANES), BF16), pltpu.VMEM((padded, LANES), BF16),
                        pltpu.VMEM((seq, LANES), BF16),
                        pltpu.VMEM((padded, LANES), BF16), pltpu.VMEM((padded, LANES), BF16),
                        pltpu.VMEM((seq, LANES), BF16),
                        pltpu.VMEM((padded, LANES), BF16), pltpu.VMEM((padded, LANES), BF16),
                        pltpu.VMEM((4, ATT_BQ, ATT_W), F32)]
        + [pltpu.VMEM((seq, LANES), F32)] * 6,
        compiler_params=_cparams(("arbitrary", "arbitrary"), ATTN_VMEM_LIMIT),
        name="dilated_attn",
    )(proj, proj, proj)


def _ret_body(lg_ref, q_ref, k_ref, v_ref, gate_ref, gain_ref, o_ref, ob, dec, *, seq):
    c_len = RET_CHUNK
    n_chunks = seq // c_len
    hd = pl.program_id(1)
    lf = lg_ref[0, hd]
    lb = lg_ref[1, hd]
    ri = lax.broadcasted_iota(jnp.int32, (c_len, c_len), 0).astype(F32)
    ci = lax.broadcasted_iota(jnp.int32, (c_len, c_len), 1).astype(F32)
    diff = ri - ci
    dec[0] = jnp.where(diff >= 0, jnp.exp(lf * jnp.maximum(diff, 0.0)),
                       jnp.exp(lb * jnp.maximum(-diff, 0.0)))
    dec[1] = jnp.exp(lf * (ri + 1.0))
    dec[2] = jnp.exp(lf * (c_len - 1.0 - ri))
    dec[3] = jnp.exp(lb * (c_len - ri))
    dec[4] = jnp.exp(lb * ri)
    cdf = jnp.exp(lf * c_len)
    cdb = jnp.exp(lb * c_len)
    tdot = (((0,), (0,)), ((), ()))

    def bwd(jg, state):
        for u in range(RET_UNROLL):
            j = jg * RET_UNROLL + u
            rows = pl.ds(pl.multiple_of((n_chunks - 1 - j) * c_len, c_len), c_len)
            qn = q_ref[rows, :].astype(F32)
            kn = k_ref[rows, :].astype(F32)
            ob[rows, :] = jnp.dot((qn * dec[3]).astype(BF16), state.astype(BF16),
                                  preferred_element_type=F32)
            kv = lax.dot_general((kn * dec[4]).astype(BF16), v_ref[rows, :], tdot,
                                 preferred_element_type=F32)
            state = state * cdb + kv
        return state

    lax.fori_loop(0, n_chunks // RET_UNROLL, bwd, jnp.zeros((c_len, c_len), F32))

    gain = gain_ref[...]

    def fwd(jg, state):
        row_sets = [pl.ds(pl.multiple_of((jg * RET_UNROLL + u) * c_len, c_len), c_len)
                    for u in range(RET_UNROLL)]
        states = []
        for rows in row_sets:
            states.append(state)
            kn = k_ref[rows, :].astype(F32)
            kv = lax.dot_general((kn * dec[2]).astype(BF16), v_ref[rows, :], tdot,
                                 preferred_element_type=F32)
            state = state * cdf + kv
        for rows, st in zip(row_sets, states):
            qb = q_ref[rows, :]
            vb = v_ref[rows, :]
            sc = lax.dot_general(qb, k_ref[rows, :], (((1,), (1,)), ((), ())),
                                 preferred_element_type=F32) * dec[0]
            tot = (ob[rows, :] + jnp.dot(sc.astype(BF16), vb, preferred_element_type=F32)
                   + jnp.dot((qb.astype(F32) * dec[1]).astype(BF16), st.astype(BF16),
                             preferred_element_type=F32))
            y = tot - jnp.mean(tot, axis=-1, keepdims=True)
            y = y * lax.rsqrt(jnp.mean(y * y, axis=-1, keepdims=True) + NORM_EPS) * gain
            o_ref[rows, :] = (y * _silu(gate_ref[rows, :].astype(F32))).astype(BF16)
        return state

    lax.fori_loop(0, n_chunks // RET_UNROLL, fwd, jnp.zeros((c_len, c_len), F32))


def _retention(proj, gain, lg, batch, seq):
    t = batch * seq
    spec = lambda off: pl.BlockSpec((None, seq, LANES), lambda b, hd: (off + hd, b, 0))
    return pl.pallas_call(
        functools.partial(_ret_body, seq=seq),
        grid=(batch, N_RET_HEADS),
        in_specs=[pl.BlockSpec(memory_space=pltpu.SMEM),
                  spec(12), spec(16), spec(20), spec(24),
                  pl.BlockSpec((None, 1, LANES), lambda b, hd: (hd, 0, 0))],
        out_specs=pl.BlockSpec((seq, LANES), lambda b, hd: (b, hd)),
        out_shape=jax.ShapeDtypeStruct((t, RET_WIDTH), BF16),
        scratch_shapes=[pltpu.VMEM((seq, LANES), F32),
                        pltpu.VMEM((5, RET_CHUNK, RET_CHUNK), F32)],
        compiler_params=_cparams(("arbitrary", "arbitrary")),
        name="retention",
    )(lg, proj, proj, proj, proj, gain)


def _out_proj(x_ref, a_ref, r_ref, wo_ref, ag_ref, hm_ref, rows=slice(None)):
    a = a_ref[rows, :].astype(F32)
    mean = jnp.dot((a * a).astype(BF16), hm_ref[...], preferred_element_type=F32)
    an = (a * lax.rsqrt(mean + NORM_EPS) * ag_ref[...]).astype(BF16)
    return (x_ref[rows, :]
            + jnp.dot(an, wo_ref[0:ATTN_WIDTH, :], preferred_element_type=F32)
            + jnp.dot(r_ref[rows, :], wo_ref[ATTN_WIDTH:, :], preferred_element_type=F32))


def _post_dense_body(x_ref, a_ref, r_ref, wo_ref, ag_ref, hm_ref, g_ref, w1_ref, w3_ref, w2_ref,
                     fg_ref, o_ref, *, final_norm):
    x2 = _out_proj(x_ref, a_ref, r_ref, wo_ref, ag_ref, hm_ref)
    h = _rms(x2, g_ref[...]).astype(BF16)
    ffn = None
    for c0, c1 in FF_CHUNKS:
        a = jnp.dot(h, w1_ref[:, c0:c1], preferred_element_type=F32)
        b = jnp.dot(h, w3_ref[:, c0:c1], preferred_element_type=F32)
        u = (_silu(a) * b).astype(BF16)
        d = jnp.dot(u, w2_ref[c0:c1, :], preferred_element_type=F32)
        ffn = d if ffn is None else ffn + d
    y = x2 + ffn
    if final_norm:
        y = _rms(y, fg_ref[...])
    o_ref[...] = y


def _post_dense(x2d, att, ret, wo, ag, hm, g, w1, w3, w2, fg, final_norm):
    t = x2d.shape[0]
    full = lambda a: pl.BlockSpec(a.shape, lambda i: (0,) * a.ndim)
    return pl.pallas_call(
        functools.partial(_post_dense_body, final_norm=final_norm),
        grid=(t // TM,),
        in_specs=[pl.BlockSpec((TM, D_MODEL), lambda i: (i, 0)),
                  pl.BlockSpec((TM, ATTN_WIDTH), lambda i: (i, 0)),
                  pl.BlockSpec((TM, RET_WIDTH), lambda i: (i, 0)),
                  full(wo), full(ag), full(hm), full(g), full(w1), full(w3), full(w2), full(fg)],
        out_specs=pl.BlockSpec((TM, D_MODEL), lambda i: (i, 0)),
        out_shape=jax.ShapeDtypeStruct((t, D_MODEL), F32),
        compiler_params=_cparams(("arbitrary",)),
        name="post_dense",
    )(x2d, att, ret, wo, ag, hm, g, w1, w3, w2, fg)


def _post_route_body(x_ref, a_ref, r_ref, wo_ref, ag_ref, hm_ref, g_ref, rt_ref,
                     x2_ref, h_ref, ri_ref, rf_ref, cnt_ref, base):
    @pl.when(pl.program_id(0) == 0)
    def _():
        base[...] = jnp.zeros_like(base)

    part = TM // ROUTE_PARTS
    parts = [pl.ds(k * part, part) for k in range(ROUTE_PARTS)]
    x2s = [_out_proj(x_ref, a_ref, r_ref, wo_ref, ag_ref, hm_ref, rows) for rows in parts]
    hs = [_rms(x2, g_ref[...]) for x2 in x2s]
    for rows, x2, h in zip(parts, x2s, hs):
        x2_ref[rows, :] = x2
        h_ref[rows, :] = h
    logits = []
    for h in hs:
        hi = h.astype(BF16)
        lo = (h - hi.astype(F32)).astype(BF16)
        pieces = (jnp.dot(hi, rt_ref[...], preferred_element_type=F32)
                  + jnp.dot(lo, rt_ref[...], preferred_element_type=F32))
        logits.append(pieces + pltpu.roll(pieces, LANES - N_EXPERTS, 1))
    lane = lax.broadcasted_iota(jnp.int32, (part, LANES), 1).astype(F32)
    row = lax.broadcasted_iota(jnp.int32, (part, part), 0)
    col = lax.broadcasted_iota(jnp.int32, (part, part), 1)
    earlier = jnp.where(col < row, 1.0, 0.0).astype(BF16)
    routed = []
    for lgt in logits:
        lg = jnp.where(lane < N_EXPERTS, lgt, -jnp.inf)
        m1 = jnp.max(lg, axis=1, keepdims=True)
        i1 = jnp.min(jnp.where(lg == m1, lane, float(LANES)), axis=1, keepdims=True)
        lg2 = jnp.where(lane == i1, -jnp.inf, lg)
        m2 = jnp.max(lg2, axis=1, keepdims=True)
        i2 = jnp.min(jnp.where(lg2 == m2, lane, float(LANES)), axis=1, keepdims=True)
        e = jnp.exp(m2 - m1)
        sel1 = lane == i1
        sel2 = lane == i2
        onehot = jnp.where(sel1 | sel2, 1.0, 0.0)
        within = jnp.dot(earlier, onehot.astype(BF16), preferred_element_type=F32)
        routed.append((i1, i2, 1.0 / (1.0 + e), e / (1.0 + e), sel1, sel2, onehot, within))
    running = base[...]
    for rows, (i1, i2, g1, g2, sel1, sel2, onehot, within) in zip(parts, routed):
        cnt = within + running
        rank1 = jnp.sum(jnp.where(sel1, cnt, 0.0), axis=1, keepdims=True)
        rank2 = jnp.sum(jnp.where(sel2, cnt, 0.0), axis=1, keepdims=True)
        running = running + jnp.sum(onehot, axis=0, keepdims=True)
        ri = jnp.where(lane == 0, i1, jnp.where(lane == 1, i2,
                       jnp.where(lane == 2, rank1, jnp.where(lane == 3, rank2, 0.0))))
        ri_ref[rows, :] = ri.astype(jnp.int32)
        rf_ref[rows, :] = jnp.where(lane == 0, g1, jnp.where(lane == 1, g2, 0.0))
    base[...] = running
    cnt_ref[...] = jnp.broadcast_to(running, cnt_ref.shape)


def _post_route(x2d, att, ret, wo, ag, hm, g, router_pad):
    t = x2d.shape[0]
    full = lambda a: pl.BlockSpec(a.shape, lambda i: (0,) * a.ndim)
    row = lambda w: pl.BlockSpec((TM, w), lambda i: (i, 0))
    return pl.pallas_call(
        _post_route_body,
        grid=(t // TM,),
        in_specs=[row(D_MODEL), row(ATTN_WIDTH), row(RET_WIDTH),
                  full(wo), full(ag), full(hm), full(g), full(router_pad)],
        out_specs=[row(D_MODEL), row(D_MODEL), row(LANES), row(LANES),
                   pl.BlockSpec((8, LANES), lambda i: (0, 0))],
        out_shape=[jax.ShapeDtypeStruct((t, D_MODEL), F32),
                   jax.ShapeDtypeStruct((t, D_MODEL), F32),
                   jax.ShapeDtypeStruct((t, LANES), jnp.int32),
                   jax.ShapeDtypeStruct((t, LANES), F32),
                   jax.ShapeDtypeStruct((8, LANES), F32)],
        scratch_shapes=[pltpu.VMEM((1, LANES), F32)],
        compiler_params=_cparams(("arbitrary",)),
        name="post_route",
    )(x2d, att, ret, wo, ag, hm, g, router_pad)


def _step_indices(pos_hbm, idx, sem_i):
    i = pl.program_id(0)
    slot = i % 2
    width = 2 * TM

    def fetch(step, s):
        dst = idx.at[pl.ds(pl.multiple_of(s * width, width), width)]
        return pltpu.make_async_copy(pos_hbm.at[step], dst, sem_i.at[s])

    @pl.when(i == 0)
    def _():
        fetch(0, 0).start()

    fetch(i, slot).wait()

    @pl.when(i + 1 < pl.num_programs(0))
    def _():
        fetch(i + 1, 1 - slot).start()

    return pl.multiple_of(slot * width, width)


def _scatter_body(ginfo, pos_hbm, h_ref, o_hbm, idx, zbuf, sem_i, sem_r, sem_z, *, n_sorted):
    i = pl.program_id(0)

    @pl.when(i == 0)
    def _():
        zbuf[...] = jnp.zeros_like(zbuf)
        total = ginfo[N_EXPERTS - 1]
        fills = []
        for e in range(N_EXPERTS):
            start = pl.multiple_of(jnp.maximum(ginfo[e] - TME, 0), TME)
            fills.append((ginfo[N_EXPERTS + e] > 0, start))
        for j in range(N_EXPERTS):
            start = n_sorted - (j + 1) * TME
            fills.append((start >= total, start))
        for phase in ("start", "wait"):
            for cond, start in fills:
                @pl.when(cond)
                def _(start=start, phase=phase):
                    cp = pltpu.make_async_copy(zbuf, o_hbm.at[pl.ds(start, TME), :], sem_z)
                    cp.start() if phase == "start" else cp.wait()

    slot = _step_indices(pos_hbm, idx, sem_i)

    def copies(g, u):
        t = g * DMA_GROUP + u
        src = h_ref.at[g, pl.ds(u, 1), :]
        return (pltpu.make_async_copy(src, o_hbm.at[pl.ds(idx[slot + t], 1), :], sem_r),
                pltpu.make_async_copy(src, o_hbm.at[pl.ds(idx[slot + TM + t], 1), :], sem_r))

    def issue(g, c):
        for u in range(DMA_GROUP):
            for cp in copies(g, u):
                cp.start()
        return c

    lax.fori_loop(0, TM // DMA_GROUP, issue, 0)

    def drain(g, c):
        for u in range(DMA_GROUP):
            for cp in copies(g, u):
                cp.wait()
        return c

    lax.fori_loop(0, TM // DMA_GROUP, drain, 0)


def _scatter(ginfo, pos, h, n_sorted):
    t = h.shape[0]
    return pl.pallas_call(
        functools.partial(_scatter_body, n_sorted=n_sorted),
        grid=(t // TM,),
        in_specs=[pl.BlockSpec(memory_space=pltpu.SMEM),
                  pl.BlockSpec(memory_space=pl.ANY),
                  pl.BlockSpec((TM // DMA_GROUP, DMA_GROUP, D_MODEL), lambda i: (i, 0, 0))],
        out_specs=pl.BlockSpec(memory_space=pl.ANY),
        out_shape=jax.ShapeDtypeStruct((n_sorted, D_MODEL), F32),
        scratch_shapes=[pltpu.SMEM((4 * TM,), jnp.int32),
                        pltpu.VMEM((TME, D_MODEL), F32),
                        pltpu.SemaphoreType.DMA((2,)), pltpu.SemaphoreType.DMA,
                        pltpu.SemaphoreType.DMA],
        compiler_params=_cparams(("arbitrary",)),
        name="moe_dispatch",
    )(ginfo, pos, h.reshape(t // DMA_GROUP, DMA_GROUP, D_MODEL))


def _expert_body(te_ref, nv_ref, h_ref, w1_ref, w3_ref, w2_ref, o_ref):
    nv = nv_ref[pl.program_id(0)]

    @pl.when(nv == 0)
    def _():
        o_ref[...] = jnp.zeros_like(o_ref)

    @pl.when(nv > 0)
    def _():
        rows = lax.broadcasted_iota(jnp.int32, (TME, D_MODEL), 0)
        h = jnp.where(rows < nv, h_ref[...], 0.0).astype(BF16)
        ffn = None
        for c0, c1 in FF_CHUNKS:
            a = jnp.dot(h, w1_ref[:, c0:c1], preferred_element_type=F32)
            b = jnp.dot(h, w3_ref[:, c0:c1], preferred_element_type=F32)
            u = (_silu(a) * b).astype(BF16)
            d = jnp.dot(u, w2_ref[c0:c1, :], preferred_element_type=F32)
            ffn = d if ffn is None else ffn + d
        o_ref[...] = ffn


def _experts(tile_e, tile_nv, hs, w1, w3, w2):
    n_sorted = hs.shape[0]
    grid_spec = pltpu.PrefetchScalarGridSpec(
        num_scalar_prefetch=2,
        grid=(n_sorted // TME,),
        in_specs=[pl.BlockSpec((TME, D_MODEL), lambda j, te, nv: (j, 0)),
                  pl.BlockSpec((None, D_MODEL, D_FF), lambda j, te, nv: (te[j], 0, 0)),
                  pl.BlockSpec((None, D_MODEL, D_FF), lambda j, te, nv: (te[j], 0, 0)),
                  pl.BlockSpec((None, D_FF, D_MODEL), lambda j, te, nv: (te[j], 0, 0))],
        out_specs=pl.BlockSpec((TME, D_MODEL), lambda j, te, nv: (j, 0)),
    )
    return pl.pallas_call(
        _expert_body,
        grid_spec=grid_spec,
        out_shape=jax.ShapeDtypeStruct((n_sorted, D_MODEL), F32),
        compiler_params=_cparams(("arbitrary",)),
        name="moe_experts",
    )(tile_e, tile_nv, hs, w1, w3, w2)


def _combine_body(pos_hbm, x_ref, rf_ref, fg_ref, y_hbm, o_ref, idx, b1, b2, sem_i, sem_r,
                  *, final_norm):
    slot = _step_indices(pos_hbm, idx, sem_i)

    def copies(g, u):
        t = g * DMA_GROUP + u
        return (pltpu.make_async_copy(y_hbm.at[pl.ds(idx[slot + t], 1), :],
                                      b1.at[g, pl.ds(u, 1), :], sem_r),
                pltpu.make_async_copy(y_hbm.at[pl.ds(idx[slot + TM + t], 1), :],
                                      b2.at[g, pl.ds(u, 1), :], sem_r))

    def issue(g, c):
        for u in range(DMA_GROUP):
            for cp in copies(g, u):
                cp.start()
        return c

    lax.fori_loop(0, TM // DMA_GROUP, issue, 0)

    def drain(g, c):
        for u in range(DMA_GROUP):
            for cp in copies(g, u):
                cp.wait()
        return c

    lax.fori_loop(0, TM // DMA_GROUP, drain, 0)

    g1 = rf_ref[:, 0:1]
    g2 = rf_ref[:, 1:2]
    y = x_ref[...] + (g1 * b1[...].reshape(TM, D_MODEL) + g2 * b2[...].reshape(TM, D_MODEL))
    if final_norm:
        y = _rms(y, fg_ref[...])
    o_ref[...] = y


def _combine(pos, x2, rf, fg, ys, final_norm):
    t = x2.shape[0]
    return pl.pallas_call(
        functools.partial(_combine_body, final_norm=final_norm),
        grid=(t // TM,),
        in_specs=[pl.BlockSpec(memory_space=pl.ANY),
                  pl.BlockSpec((TM, D_MODEL), lambda i: (i, 0)),
                  pl.BlockSpec((TM, LANES), lambda i: (i, 0)),
                  pl.BlockSpec((1, D_MODEL), lambda i: (0, 0)),
                  pl.BlockSpec(memory_space=pl.ANY)],
        out_specs=pl.BlockSpec((TM, D_MODEL), lambda i: (i, 0)),
        out_shape=jax.ShapeDtypeStruct((t, D_MODEL), F32),
        scratch_shapes=[pltpu.SMEM((4 * TM,), jnp.int32),
                        pltpu.VMEM((TM // DMA_GROUP, DMA_GROUP, D_MODEL), F32),
                        pltpu.VMEM((TM // DMA_GROUP, DMA_GROUP, D_MODEL), F32),
                        pltpu.SemaphoreType.DMA((2,)), pltpu.SemaphoreType.DMA],
        compiler_params=_cparams(("arbitrary",)),
        name="moe_combine",
    )(pos, x2, rf, fg, ys)


def _moe(x2d, att, ret, wo, ag, hm, g, router_pad, w1, w3, w2, fg, final_norm):
    t = x2d.shape[0]
    n_sorted = 2 * t + N_EXPERTS * TME
    n_tiles = n_sorted // TME
    x2, h, ri, rf, cnt = _post_route(x2d, att, ret, wo, ag, hm, g, router_pad)

    counts = cnt[0, :N_EXPERTS].astype(jnp.int32)
    padded = ((counts + TME - 1) // TME) * TME
    ends = jnp.cumsum(padded)
    starts = ends - padded
    experts = jnp.arange(N_EXPERTS, dtype=jnp.int32)
    start_of = lambda e: jnp.sum(jnp.where(e[:, None] == experts[None, :], starts[None, :], 0), axis=1)
    pos1 = start_of(ri[:, 0]) + ri[:, 2]
    pos2 = start_of(ri[:, 1]) + ri[:, 3]
    pos = jnp.concatenate([pos1.reshape(t // TM, TM), pos2.reshape(t // TM, TM)], axis=1)
    tile_start = jnp.arange(n_tiles, dtype=jnp.int32) * TME
    tile_e = jnp.minimum(jnp.sum(tile_start[:, None] >= ends[None, :], axis=1), N_EXPERTS - 1)
    tile_e = tile_e.astype(jnp.int32)
    tile_nv = jnp.clip(starts[tile_e] + counts[tile_e] - tile_start, 0, TME)
    tile_nv = jnp.where(tile_start < ends[-1], tile_nv, 0).astype(jnp.int32)
    ginfo = jnp.concatenate([ends, padded]).astype(jnp.int32)

    hs = _scatter(ginfo, pos, h, n_sorted)
    ys = _experts(tile_e, tile_nv, hs, w1, w3, w2)
    return _combine(pos, x2, rf, fg, ys, final_norm)


def _rope_tables(seq):
    pos = jnp.arange(seq, dtype=F32)[:, None]
    lane = jnp.arange(LANES)
    j = lane % ATTN_HEAD_DIM
    half = ROPE_DIM // 2
    inv_a = ROPE_THETA ** (-jnp.arange(half, dtype=F32) * 2.0 / ROPE_DIM)
    ang = pos * inv_a[None, :]
    cos_a, sin_a = jnp.cos(ang), jnp.sin(ang)
    jc = jnp.where(j < half, j, j - half) % half
    ca = jnp.where(j[None, :] < ROPE_DIM, cos_a[:, jc], 1.0)
    sa1 = jnp.where(((j >= half) & (j < ROPE_DIM))[None, :], sin_a[:, jc], 0.0)
    sa2 = jnp.where((j < half)[None, :], -sin_a[:, jc], 0.0)
    half_r = RET_HEAD_DIM // 2
    inv_r = RET_THETA ** (-jnp.arange(half_r, dtype=F32) * 2.0 / RET_HEAD_DIM)
    ang_r = pos * inv_r[None, :]
    cos_r, sin_r = jnp.cos(ang_r), jnp.sin(ang_r)
    cr = cos_r[:, lane % half_r]
    sr = jnp.where((lane < half_r)[None, :], -sin_r[:, lane % half_r], sin_r[:, lane % half_r])
    return ca, sa1, sa2, cr, sr


def _pack_router(router):
    hi = router.astype(BF16)
    lo = (router - hi.astype(F32)).astype(BF16)
    packed = jnp.concatenate([hi, lo], axis=-1)
    return jnp.pad(packed, ((0, 0), (0, 0), (0, LANES - 2 * N_EXPERTS)))


def _trunk(x, w):
    batch, seq, _ = x.shape
    t = batch * seq
    assert seq % (16 * ATT_BQ) == 0 and seq % TM == 0 and t % TM == 0
    tabs = _rope_tables(seq)
    x2d = x.reshape(t, D_MODEL)
    depth = w["w_in"].shape[0]
    head_of = jnp.arange(ATTN_WIDTH) // ATTN_HEAD_DIM
    head_mean = jnp.where(head_of[:, None] == head_of[None, :], 1.0 / ATTN_HEAD_DIM, 0.0).astype(BF16)
    for layer in range(depth):
        last = layer == depth - 1
        ag = w["attn_out_g"][layer][None, :]
        proj = _in_proj(x2d, w["attn_norm_g"][layer][None, :], w["w_in"][layer], tabs, seq)
        att = _attention(proj, batch, seq)
        ret = _retention(proj, w["ret_out_g"][layer].reshape(N_RET_HEADS, 1, LANES),
                         w["lg"][layer], batch, seq)
        i = layer // 2
        if layer % 2 == 0:
            x2d = _post_dense(x2d, att, ret, w["w_out"][layer], ag, head_mean,
                              w["ffn_norm_g"][layer][None, :],
                              w["dense_w1"][i], w["dense_w3"][i], w["dense_w2"][i],
                              w["final_norm_g"], last)
        else:
            x2d = _moe(x2d, att, ret, w["w_out"][layer], ag, head_mean,
                       w["ffn_norm_g"][layer][None, :],
                       w["router_pad"][i], w["moe_w1"][i], w["moe_w3"][i], w["moe_w2"][i],
                       w["final_norm_g"], last)
    return x2d.reshape(batch, seq, D_MODEL)


def kernel(x_prompt, x_sample, attn_norm_g, w_in, attn_out_g, ret_out_g, ret_decay_fwd, ret_decay_bwd, w_out, ffn_norm_g, dense_w1, dense_w3, dense_w2, moe_router, moe_w1, moe_w3, moe_w2, final_norm_g):
    lg_f = jnp.log1p(-jnp.exp(ret_decay_fwd.astype(F32)))
    lg_b = jnp.log1p(-jnp.exp(ret_decay_bwd.astype(F32)))
    w = {
        "attn_norm_g": attn_norm_g, "attn_out_g": attn_out_g, "ret_out_g": ret_out_g,
        "ffn_norm_g": ffn_norm_g, "final_norm_g": final_norm_g[None, :],
        "lg": jnp.stack([lg_f, lg_b], axis=1),
        "w_in": w_in.astype(BF16), "w_out": w_out.astype(BF16),
        "dense_w1": dense_w1.astype(BF16), "dense_w3": dense_w3.astype(BF16),
        "dense_w2": dense_w2.astype(BF16),
        "moe_w1": moe_w1.astype(BF16), "moe_w3": moe_w3.astype(BF16), "moe_w2": moe_w2.astype(BF16),
        "router_pad": _pack_router(moe_router.astype(F32)),
    }
    return _trunk(x_prompt, w), _trunk(x_sample, w)
```

```python
import functools

import jax
import jax.numpy as jnp
from jax import lax
from jax.experimental import pallas as pl
from jax.experimental.pallas import tpu as pltpu

F32 = jnp.float32
BF16 = jnp.bfloat16

D_MODEL = 1024
N_ATTN_HEADS = 8
ATTN_HEAD_DIM = 64
ATTN_WIDTH = N_ATTN_HEADS * ATTN_HEAD_DIM
ATTN_RADIUS = 64
ROPE_THETA = 500000.0
ROPE_DIM = ATTN_HEAD_DIM // 4
N_RET_HEADS = 4
RET_HEAD_DIM = 128
RET_WIDTH = N_RET_HEADS * RET_HEAD_DIM
RET_THETA = 10000.0
RET_CHUNK = 128
IN_COLS = 3 * ATTN_WIDTH + 4 * RET_WIDTH
N_EXPERTS = 8
D_FF = 2816
NORM_EPS = 1e-6
NEG_INF = -1e30
TINY = 1e-30

LANES = 128
N_COL_BLOCKS = IN_COLS // LANES
VMEM_LIMIT = 56 * 1024 * 1024
ATTN_VMEM_LIMIT = 58 * 1024 * 1024

TM = 512
TME = 512
TMC = 1024
FF_CHUNKS = ((0, 1024), (1024, 2048), (2048, 2816))
ATT_BQ = 128
ATT_W = ATT_BQ + 2 * ATTN_RADIUS
ROUTE_PARTS = 2
ATT_NB = 8
RET_UNROLL = 8
DMA_GROUP = 8


def _cparams(sem, vmem_limit=VMEM_LIMIT):
    return pltpu.CompilerParams(dimension_semantics=sem, vmem_limit_bytes=vmem_limit)


def _rms(x, g):
    return x * lax.rsqrt(jnp.mean(x * x, axis=-1, keepdims=True) + NORM_EPS) * g


def _silu(x):
    return x / (1.0 + jnp.exp(-x))


def _in_proj_body(x_ref, g_ref, w_ref, ca_ref, sa1_ref, sa2_ref, cr_ref, sr_ref, o_ref):
    h = _rms(x_ref[...], g_ref[...]).astype(BF16)
    for c in range(7):
        acc = jnp.dot(h, w_ref[:, c * 512:(c + 1) * 512], preferred_element_type=F32)
        for s in range(4):
            a = acc[:, s * LANES:(s + 1) * LANES]
            if c in (0, 1):
                a = (a * ca_ref[...] + pltpu.roll(a, ROPE_DIM // 2, 1) * sa1_ref[...]
                     + pltpu.roll(a, LANES - ROPE_DIM // 2, 1) * sa2_ref[...])
                if c == 0:
                    a = a * (ATTN_HEAD_DIM ** -0.5)
            elif c in (3, 4):
                a = a * cr_ref[...] + pltpu.roll(a, RET_HEAD_DIM // 2, 1) * sr_ref[...]
                if c == 4:
                    a = a * (RET_HEAD_DIM ** -0.5)
            o_ref[c * 4 + s] = a.astype(BF16)


def _in_proj(x2d, g, w, tabs, seq):
    t = x2d.shape[0]
    nblk = seq // TM
    tab_spec = pl.BlockSpec((TM, LANES), lambda i: (i % nblk, 0))
    return pl.pallas_call(
        _in_proj_body,
        grid=(t // TM,),
        in_specs=[pl.BlockSpec((TM, D_MODEL), lambda i: (i, 0)),
                  pl.BlockSpec((1, D_MODEL), lambda i: (0, 0)),
                  pl.BlockSpec((D_MODEL, IN_COLS), lambda i: (0, 0)),
                  tab_spec, tab_spec, tab_spec, tab_spec, tab_spec],
        out_specs=pl.BlockSpec((N_COL_BLOCKS, TM, LANES), lambda i: (0, i, 0)),
        out_shape=jax.ShapeDtypeStruct((N_COL_BLOCKS, t, LANES), BF16),
        compiler_params=_cparams(("arbitrary",)),
        name="in_proj",
    )(x2d, g, w, *tabs)


def _attn_body(q_ref, k_ref, v_ref, o_ref,
               k1, v1, q4, k4, v4, q16, k16, v16, bias,
               acc_a, ms_a, ls_a, acc_b, ms_b, ls_b, *, seq):
    st_a = (acc_a, ms_a, ls_a)
    st_b = (acc_b, ms_b, ls_b)
    stage = 512
    pad = ATTN_RADIUS
    lane = lax.broadcasted_iota(jnp.int32, (ATT_BQ, LANES), 1)
    head0 = lane < ATTN_HEAD_DIM
    khead0 = lax.broadcasted_iota(jnp.int32, (ATT_W, LANES), 1) < ATTN_HEAD_DIM

    qi = lax.broadcasted_iota(jnp.int32, (ATT_BQ, ATT_W), 0)
    kj = lax.broadcasted_iota(jnp.int32, (ATT_BQ, ATT_W), 1)
    band = jnp.abs(kj - pad - qi) <= ATTN_RADIUS
    for e in range(4):
        ok = band
        if e & 1:
            ok = ok & (kj >= pad)
        if e & 2:
            ok = ok & (kj < ATT_BQ + pad)
        bias[e] = jnp.where(ok, 0.0, NEG_INF)

    for arr in (k1, v1, k4, v4, k16, v16):
        arr[pl.ds(0, pad), :] = jnp.zeros((pad, LANES), BF16)
        arr[pl.ds(seq + pad, pad), :] = jnp.zeros((pad, LANES), BF16)

    cls4, cls16 = seq // 4, seq // 16
    xf, x4f = st_a[0], st_b[0]
    for src, d1, d4, d16 in ((q_ref, None, q4, q16), (k_ref, k1, k4, k16), (v_ref, v1, v4, v16)):
        off = 0 if d1 is None else pad

        def upcast(i, c, src=src, d1=d1):
            r0 = pl.multiple_of(i * stage, stage)
            x = src[pl.ds(r0, stage), :]
            xf[pl.ds(r0, stage), :] = x.astype(F32)
            if d1 is not None:
                d1[pl.ds(pad + r0, stage), :] = x
            return c
        lax.fori_loop(0, seq // stage, upcast, 0)
        ch = min(cls4, 256)
        for a in range(4):
            def to4(i, c, d4=d4, a=a, ch=ch, off=off):
                j0 = pl.multiple_of(i * ch, ch)
                x = xf[pl.ds(a + 4 * j0, ch, stride=4), :]
                x4f[pl.ds(a * cls4 + j0, ch), :] = x
                d4[pl.ds(off + a * cls4 + j0, ch), :] = x.astype(BF16)
                return c
            lax.fori_loop(0, cls4 // ch, to4, 0)
        ch = min(cls16, 256)
        for a in range(4):
            for b in range(4):
                def to16(i, c, d16=d16, a=a, b=b, ch=ch, off=off):
                    i0 = pl.multiple_of(i * ch, ch)
                    d16[pl.ds(off + a * cls4 + b * cls16 + i0, ch), :] = (
                        x4f[pl.ds(a * cls4 + b + 4 * i0, ch, stride=4), :].astype(BF16))
                    return c
                lax.fori_loop(0, cls16 // ch, to16, 0)

    def relayout_16_to_4(src, dst):
        ch = min(cls16, ATT_BQ)
        for a in range(4):
            for b in range(4):
                def body(i, c, a=a, b=b):
                    i0 = pl.multiple_of(i * ch, ch)
                    for s_, d_ in zip(src, dst):
                        d_[pl.ds(a * cls4 + b + 4 * i0, ch, stride=4), :] = (
                            s_[pl.ds(a * cls4 + b * cls16 + i0, ch), :])
                    return c
                lax.fori_loop(0, cls16 // ch, body, 0)

    def relayout_4_to_1(src, dst):
        ch = ATT_BQ
        for a in range(4):
            def body(i, c, a=a):
                j0 = pl.multiple_of(i * ch, ch)
                for s_, d_ in zip(src, dst):
                    d_[pl.ds(a + 4 * j0, ch, stride=4), :] = s_[pl.ds(a * cls4 + j0, ch), :]
                return c
            lax.fori_loop(0, cls4 // ch, body, 0)

    def branch(d, qd, kd, vd, state, mode):
        acc, ms, ls = state
        nb = ATT_NB
        nblk = seq // d // ATT_BQ

        def group(g, c):
            blocks = []
            for u in range(nb):
                t = g * nb + u
                base = pl.multiple_of(t * ATT_BQ, ATT_BQ)
                n = t & (nblk - 1)
                edge = jnp.where(n == 0, 1, 0) + jnp.where(n == nblk - 1, 2, 0)
                blocks.append((base, edge))
            scores = []
            for base, edge in blocks:
                qb = qd[pl.ds(base, ATT_BQ), :]
                kw = kd[pl.ds(base, ATT_W), :]
                for hd in range(2):
                    qh = jnp.where(head0 if hd == 0 else ~head0, qb, jnp.zeros_like(qb))
                    scores.append(lax.dot_general(qh, kw, (((1,), (1,)), ((), ())),
                                                  preferred_element_type=F32) + bias[edge])
            maxes = [jnp.max(s, axis=1, keepdims=True) for s in scores]
            probs = [jnp.exp(s - m).astype(BF16) for s, m in zip(scores, maxes)]
            fulls = []
            for i, (base, edge) in enumerate(blocks):
                vw = vd[pl.ds(base, ATT_W), :]
                for hd in range(2):
                    vh = jnp.where(khead0 if hd == 0 else ~khead0, vw, jnp.ones_like(vw))
                    fulls.append(jnp.dot(probs[2 * i + hd], vh, preferred_element_type=F32))
            for i, (base, edge) in enumerate(blocks):
                f0, f1 = fulls[2 * i], fulls[2 * i + 1]
                o_new = jnp.where(head0, f0, f1)
                l_new = pltpu.roll(jnp.where(head0, f1, f0), ATTN_HEAD_DIM, 1)
                l_new = jnp.maximum(l_new, TINY)
                m_new = jnp.where(head0, maxes[2 * i], maxes[2 * i + 1])
                rows = pl.ds(base, ATT_BQ)
                if mode == "first":
                    acc[rows, :] = o_new
                    ms[rows, :] = m_new
                    ls[rows, :] = l_new
                    continue
                m_old = ms[rows, :]
                m_tot = jnp.maximum(m_old, m_new)
                a_old = jnp.exp(m_old - m_tot)
                a_new = jnp.exp(m_new - m_tot)
                o_tot = acc[rows, :] * a_old + o_new * a_new
                l_tot = ls[rows, :] * a_old + l_new * a_new
                if mode == "merge":
                    acc[rows, :] = o_tot
                    ls[rows, :] = l_tot
                    ms[rows, :] = m_tot
                else:
                    o_ref[rows, :] = (o_tot / l_tot).astype(BF16)
            return c

        lax.fori_loop(0, seq // (ATT_BQ * nb), group, 0)

    branch(16, q16, k16, v16, st_a, "first")
    relayout_16_to_4(st_a, st_b)
    branch(4, q4, k4, v4, st_b, "merge")
    relayout_4_to_1(st_b, st_a)
    branch(1, q_ref, k1, v1, st_a, "last")


def _attention(proj, batch, seq):
    t = batch * seq
    padded = seq + 2 * ATTN_RADIUS
    qkv_spec = lambda off, mode=None: pl.BlockSpec(
        (None, seq, LANES), lambda b, hp: (off + hp, b, 0), pipeline_mode=mode)
    return pl.pallas_call(
        functools.partial(_attn_body, seq=seq),
        grid=(batch, ATTN_WIDTH // LANES),
        in_specs=[qkv_spec(0), qkv_spec(4), qkv_spec(8, pl.Buffered(1))],
        out_specs=pl.BlockSpec((seq, LANES), lambda b, hp: (b, hp)),
        out_shape=jax.ShapeDtypeStruct((t, ATTN_WIDTH), BF16),
        scratch_shapes=[pltpu.VMEM((padded, LANES), BF16), pltpu.VMEM((padded, LANES), BF16),
                        pltpu.VMEM((seq, LANES), BF16),
                        pltpu.VMEM((padded, LANES), BF16), pltpu.VMEM((padded, LANES), BF16),
                        pltpu.VMEM((seq, LANES), BF16),
                        pltpu.VMEM((padded, LANES), BF16), pltpu.VMEM((padded, LANES), BF16),
                        pltpu.VMEM((4, ATT_BQ, ATT_W), F32)]
        + [pltpu.VMEM((seq, LANES), F32)] * 6,
        compiler_params=_cparams(("arbitrary", "arbitrary"), ATTN_VMEM_LIMIT),
        name="dilated_attn",
    )(proj, proj, proj)


def _ret_body(lg_ref, q_ref, k_ref, v_ref, gate_ref, gain_ref, o_ref, ob, dec, *, seq):
    c_len = RET_CHUNK
    n_chunks = seq // c_len
    hd = pl.program_id(1)
    lf = lg_ref[0, hd]
    lb = lg_ref[1, hd]
    ri = lax.broadcasted_iota(jnp.int32, (c_len, c_len), 0).astype(F32)
    ci = lax.broadcasted_iota(jnp.int32, (c_len, c_len), 1).astype(F32)
    diff = ri - ci
    dec[0] = jnp.where(diff >= 0, jnp.exp(lf * jnp.maximum(diff, 0.0)),
                       jnp.exp(lb * jnp.maximum(-diff, 0.0)))
    dec[1] = jnp.exp(lf * (ri + 1.0))
    dec[2] = jnp.exp(lf * (c_len - 1.0 - ri))
    dec[3] = jnp.exp(lb * (c_len - ri))
    dec[4] = jnp.exp(lb * ri)
    cdf = jnp.exp(lf * c_len)
    cdb = jnp.exp(lb * c_len)
    tdot = (((0,), (0,)), ((), ()))

    def bwd(jg, state):
        row_sets = [pl.ds(pl.multiple_of((n_chunks - 1 - (jg * RET_UNROLL + u)) * c_len, c_len), c_len)
                    for u in range(RET_UNROLL)]
        states = []
        for rows in row_sets:
            states.append(state)
            kn = k_ref[rows, :].astype(F32)
            kv = lax.dot_general((kn * dec[4]).astype(BF16), v_ref[rows, :], tdot,
                                 preferred_element_type=F32)
            state = state * cdb + kv
        for rows, st in zip(row_sets, states):
            qn = q_ref[rows, :].astype(F32)
            ob[rows, :] = jnp.dot((qn * dec[3]).astype(BF16), st.astype(BF16),
                                  preferred_element_type=F32)
        return state

    lax.fori_loop(0, n_chunks // RET_UNROLL, bwd, jnp.zeros((c_len, c_len), F32))

    gain = gain_ref[...]

    def fwd(jg, state):
        row_sets = [pl.ds(pl.multiple_of((jg * RET_UNROLL + u) * c_len, c_len), c_len)
                    for u in range(RET_UNROLL)]
        states = []
        for rows in row_sets:
            states.append(state)
            kn = k_ref[rows, :].astype(F32)
            kv = lax.dot_general((kn * dec[2]).astype(BF16), v_ref[rows, :], tdot,
                                 preferred_element_type=F32)
            state = state * cdf + kv
        for rows, st in zip(row_sets, states):
            qb = q_ref[rows, :]
            vb = v_ref[rows, :]
            sc = lax.dot_general(qb, k_ref[rows, :], (((1,), (1,)), ((), ())),
                                 preferred_element_type=F32) * dec[0]
            tot = (ob[rows, :] + jnp.dot(sc.astype(BF16), vb, preferred_element_type=F32)
                   + jnp.dot((qb.astype(F32) * dec[1]).astype(BF16), st.astype(BF16),
                             preferred_element_type=F32))
            y = tot - jnp.mean(tot, axis=-1, keepdims=True)
            y = y * lax.rsqrt(jnp.mean(y * y, axis=-1, keepdims=True) + NORM_EPS) * gain
            o_ref[rows, :] = (y * _silu(gate_ref[rows, :].astype(F32))).astype(BF16)
        return state

    lax.fori_loop(0, n_chunks // RET_UNROLL, fwd, jnp.zeros((c_len, c_len), F32))


def _retention(proj, gain, lg, batch, seq):
    t = batch * seq
    spec = lambda off: pl.BlockSpec((None, seq, LANES), lambda b, hd: (off + hd, b, 0))
    return pl.pallas_call(
        functools.partial(_ret_body, seq=seq),
        grid=(batch, N_RET_HEADS),
        in_specs=[pl.BlockSpec(memory_space=pltpu.SMEM),
                  spec(12), spec(16), spec(20), spec(24),
                  pl.BlockSpec((None, 1, LANES), lambda b, hd: (hd, 0, 0))],
        out_specs=pl.BlockSpec((seq, LANES), lambda b, hd: (b, hd)),
        out_shape=jax.ShapeDtypeStruct((t, RET_WIDTH), BF16),
        scratch_shapes=[pltpu.VMEM((seq, LANES), F32),
                        pltpu.VMEM((5, RET_CHUNK, RET_CHUNK), F32)],
        compiler_params=_cparams(("arbitrary", "arbitrary")),
        name="retention",
    )(lg, proj, proj, proj, proj, gain)


def _out_proj(x_ref, a_ref, r_ref, wo_ref, ag_ref, hm_ref, rows=slice(None)):
    a = a_ref[rows, :].astype(F32)
    mean = jnp.dot((a * a).astype(BF16), hm_ref[...], preferred_element_type=F32)
    an = (a * lax.rsqrt(mean + NORM_EPS) * ag_ref[...]).astype(BF16)
    return (x_ref[rows, :]
            + jnp.dot(an, wo_ref[0:ATTN_WIDTH, :], preferred_element_type=F32)
            + jnp.dot(r_ref[rows, :], wo_ref[ATTN_WIDTH:, :], preferred_element_type=F32))


def _post_dense_body(x_ref, a_ref, r_ref, wo_ref, ag_ref, hm_ref, g_ref, w1_ref, w3_ref, w2_ref,
                     fg_ref, o_ref, *, final_norm):
    x2 = _out_proj(x_ref, a_ref, r_ref, wo_ref, ag_ref, hm_ref)
    h = _rms(x2, g_ref[...]).astype(BF16)
    ffn = None
    for c0, c1 in FF_CHUNKS:
        a = jnp.dot(h, w1_ref[:, c0:c1], preferred_element_type=F32)
        b = jnp.dot(h, w3_ref[:, c0:c1], preferred_element_type=F32)
        u = (_silu(a) * b).astype(BF16)
        d = jnp.dot(u, w2_ref[c0:c1, :], preferred_element_type=F32)
        ffn = d if ffn is None else ffn + d
    y = x2 + ffn
    if final_norm:
        y = _rms(y, fg_ref[...])
    o_ref[...] = y


def _post_dense(x2d, att, ret, wo, ag, hm, g, w1, w3, w2, fg, final_norm):
    t = x2d.shape[0]
    full = lambda a: pl.BlockSpec(a.shape, lambda i: (0,) * a.ndim)
    return pl.pallas_call(
        functools.partial(_post_dense_body, final_norm=final_norm),
        grid=(t // TM,),
        in_specs=[pl.BlockSpec((TM, D_MODEL), lambda i: (i, 0)),
                  pl.BlockSpec((TM, ATTN_WIDTH), lambda i: (i, 0)),
                  pl.BlockSpec((TM, RET_WIDTH), lambda i: (i, 0)),
                  full(wo), full(ag), full(hm), full(g), full(w1), full(w3), full(w2), full(fg)],
        out_specs=pl.BlockSpec((TM, D_MODEL), lambda i: (i, 0)),
        out_shape=jax.ShapeDtypeStruct((t, D_MODEL), F32),
        compiler_params=_cparams(("arbitrary",)),
        name="post_dense",
    )(x2d, att, ret, wo, ag, hm, g, w1, w3, w2, fg)


def _post_route_body(x_ref, a_ref, r_ref, wo_ref, ag_ref, hm_ref, g_ref, rt_ref,
                     x2_ref, h_ref, ri_ref, rf_ref, cnt_ref, base):
    @pl.when(pl.program_id(0) == 0)
    def _():
        base[...] = jnp.zeros_like(base)

    part = TM // ROUTE_PARTS
    parts = [pl.ds(k * part, part) for k in range(ROUTE_PARTS)]
    x2s = [_out_proj(x_ref, a_ref, r_ref, wo_ref, ag_ref, hm_ref, rows) for rows in parts]
    hs = [_rms(x2, g_ref[...]) for x2 in x2s]
    for rows, x2, h in zip(parts, x2s, hs):
        x2_ref[rows, :] = x2
        h_ref[rows, :] = h
    logits = []
    for h in hs:
        hi = h.astype(BF16)
        lo = (h - hi.astype(F32)).astype(BF16)
        pieces = (jnp.dot(hi, rt_ref[...], preferred_element_type=F32)
                  + jnp.dot(lo, rt_ref[...], preferred_element_type=F32))
        logits.append(pieces + pltpu.roll(pieces, LANES - N_EXPERTS, 1))
    lane = lax.broadcasted_iota(jnp.int32, (part, LANES), 1).astype(F32)
    row = lax.broadcasted_iota(jnp.int32, (part, part), 0)
    col = lax.broadcasted_iota(jnp.int32, (part, part), 1)
    earlier = jnp.where(col < row, 1.0, 0.0).astype(BF16)
    routed = []
    for lgt in logits:
        lg = jnp.where(lane < N_EXPERTS, lgt, -jnp.inf)
        m1 = jnp.max(lg, axis=1, keepdims=True)
        i1 = jnp.min(jnp.where(lg == m1, lane, float(LANES)), axis=1, keepdims=True)
        lg2 = jnp.where(lane == i1, -jnp.inf, lg)
        m2 = jnp.max(lg2, axis=1, keepdims=True)
        i2 = jnp.min(jnp.where(lg2 == m2, lane, float(LANES)), axis=1, keepdims=True)
        e = jnp.exp(m2 - m1)
        sel1 = lane == i1
        sel2 = lane == i2
        onehot = jnp.where(sel1 | sel2, 1.0, 0.0)
        within = jnp.dot(earlier, onehot.astype(BF16), preferred_element_type=F32)
        routed.append((i1, i2, 1.0 / (1.0 + e), e / (1.0 + e), sel1, sel2, onehot, within))
    running = base[...]
    for rows, (i1, i2, g1, g2, sel1, sel2, onehot, within) in zip(parts, routed):
        cnt = within + running
        rank1 = jnp.sum(jnp.where(sel1, cnt, 0.0), axis=1, keepdims=True)
        rank2 = jnp.sum(jnp.where(sel2, cnt, 0.0), axis=1, keepdims=True)
        running = running + jnp.sum(onehot, axis=0, keepdims=True)
        ri = jnp.where(lane == 0, i1, jnp.where(lane == 1, i2,
                       jnp.where(lane == 2, rank1, jnp.where(lane == 3, rank2, 0.0))))
        ri_ref[rows, :] = ri.astype(jnp.int32)
        rf_ref[rows, :] = jnp.where(lane == 0, g1, jnp.where(lane == 1, g2, 0.0))
    base[...] = running
    cnt_ref[...] = jnp.broadcast_to(running, cnt_ref.shape)


def _post_route(x2d, att, ret, wo, ag, hm, g, router_pad):
    t = x2d.shape[0]
    full = lambda a: pl.BlockSpec(a.shape, lambda i: (0,) * a.ndim)
    row = lambda w: pl.BlockSpec((TM, w), lambda i: (i, 0))
    return pl.pallas_call(
        _post_route_body,
        grid=(t // TM,),
        in_specs=[row(D_MODEL), row(ATTN_WIDTH), row(RET_WIDTH),
                  full(wo), full(ag), full(hm), full(g), full(router_pad)],
        out_specs=[row(D_MODEL), row(D_MODEL), row(LANES), row(LANES),
                   pl.BlockSpec((8, LANES), lambda i: (0, 0))],
        out_shape=[jax.ShapeDtypeStruct((t, D_MODEL), F32),
                   jax.ShapeDtypeStruct((t, D_MODEL), F32),
                   jax.ShapeDtypeStruct((t, LANES), jnp.int32),
                   jax.ShapeDtypeStruct((t, LANES), F32),
                   jax.ShapeDtypeStruct((8, LANES), F32)],
        scratch_shapes=[pltpu.VMEM((1, LANES), F32)],
        compiler_params=_cparams(("arbitrary",)),
        name="post_route",
    )(x2d, att, ret, wo, ag, hm, g, router_pad)


def _step_indices(pos_hbm, idx, sem_i):
    i = pl.program_id(0)
    slot = i % 2
    width = 2 * TMC

    def fetch(step, s):
        dst = idx.at[pl.ds(pl.multiple_of(s * width, width), width)]
        return pltpu.make_async_copy(pos_hbm.at[step], dst, sem_i.at[s])

    @pl.when(i == 0)
    def _():
        fetch(0, 0).start()

    fetch(i, slot).wait()

    @pl.when(i + 1 < pl.num_programs(0))
    def _():
        fetch(i + 1, 1 - slot).start()

    return pl.multiple_of(slot * width, width)


def _scatter_body(ginfo, pos_hbm, h_ref, o_hbm, idx, zbuf, sem_i, sem_r, sem_z, *, n_sorted):
    i = pl.program_id(0)

    @pl.when(i == 0)
    def _():
        zbuf[...] = jnp.zeros_like(zbuf)
        total = ginfo[N_EXPERTS - 1]
        fills = []
        for e in range(N_EXPERTS):
            start = pl.multiple_of(jnp.maximum(ginfo[e] - TME, 0), TME)
            fills.append((ginfo[N_EXPERTS + e] > 0, start))
        for j in range(N_EXPERTS):
            start = n_sorted - (j + 1) * TME
            fills.append((start >= total, start))
        for phase in ("start", "wait"):
            for cond, start in fills:
                @pl.when(cond)
                def _(start=start, phase=phase):
                    cp = pltpu.make_async_copy(zbuf, o_hbm.at[pl.ds(start, TME), :], sem_z)
                    cp.start() if phase == "start" else cp.wait()

    slot = _step_indices(pos_hbm, idx, sem_i)

    def copies(g, u):
        t = g * DMA_GROUP + u
        src = h_ref.at[g, pl.ds(u, 1), :]
        return (pltpu.make_async_copy(src, o_hbm.at[pl.ds(idx[slot + t], 1), :], sem_r),
                pltpu.make_async_copy(src, o_hbm.at[pl.ds(idx[slot + TMC + t], 1), :], sem_r))

    def issue(g, c):
        for u in range(DMA_GROUP):
            for cp in copies(g, u):
                cp.start()
        return c

    lax.fori_loop(0, TMC // DMA_GROUP, issue, 0)

    def drain(g, c):
        for u in range(DMA_GROUP):
            for cp in copies(g, u):
                cp.wait()
        return c

    lax.fori_loop(0, TMC // DMA_GROUP, drain, 0)


def _scatter(ginfo, pos, h, n_sorted):
    t = h.shape[0]
    return pl.pallas_call(
        functools.partial(_scatter_body, n_sorted=n_sorted),
        grid=(t // TMC,),
        in_specs=[pl.BlockSpec(memory_space=pltpu.SMEM),
                  pl.BlockSpec(memory_space=pl.ANY),
                  pl.BlockSpec((TMC // DMA_GROUP, DMA_GROUP, D_MODEL), lambda i: (i, 0, 0))],
        out_specs=pl.BlockSpec(memory_space=pl.ANY),
        out_shape=jax.ShapeDtypeStruct((n_sorted, D_MODEL), F32),
        scratch_shapes=[pltpu.SMEM((4 * TMC,), jnp.int32),
                        pltpu.VMEM((TME, D_MODEL), F32),
                        pltpu.SemaphoreType.DMA((2,)), pltpu.SemaphoreType.DMA,
                        pltpu.SemaphoreType.DMA],
        compiler_params=_cparams(("arbitrary",)),
        name="moe_dispatch",
    )(ginfo, pos, h.reshape(t // DMA_GROUP, DMA_GROUP, D_MODEL))


def _expert_body(te_ref, nv_ref, h_ref, w1_ref, w3_ref, w2_ref, o_ref):
    nv = nv_ref[pl.program_id(0)]

    @pl.when(nv == 0)
    def _():
        o_ref[...] = jnp.zeros_like(o_ref)

    @pl.when(nv > 0)
    def _():
        rows = lax.broadcasted_iota(jnp.int32, (TME, D_MODEL), 0)
        h = jnp.where(rows < nv, h_ref[...], 0.0).astype(BF16)
        ffn = None
        for c0, c1 in FF_CHUNKS:
            a = jnp.dot(h, w1_ref[:, c0:c1], preferred_element_type=F32)
            b = jnp.dot(h, w3_ref[:, c0:c1], preferred_element_type=F32)
            u = (_silu(a) * b).astype(BF16)
            d = jnp.dot(u, w2_ref[c0:c1, :], preferred_element_type=F32)
            ffn = d if ffn is None else ffn + d
        o_ref[...] = ffn


def _experts(tile_e, tile_nv, hs, w1, w3, w2):
    n_sorted = hs.shape[0]
    grid_spec = pltpu.PrefetchScalarGridSpec(
        num_scalar_prefetch=2,
        grid=(n_sorted // TME,),
        in_specs=[pl.BlockSpec((TME, D_MODEL), lambda j, te, nv: (j, 0)),
                  pl.BlockSpec((None, D_MODEL, D_FF), lambda j, te, nv: (te[j], 0, 0)),
                  pl.BlockSpec((None, D_MODEL, D_FF), lambda j, te, nv: (te[j], 0, 0)),
                  pl.BlockSpec((None, D_FF, D_MODEL), lambda j, te, nv: (te[j], 0, 0))],
        out_specs=pl.BlockSpec((TME, D_MODEL), lambda j, te, nv: (j, 0)),
    )
    return pl.pallas_call(
        _expert_body,
        grid_spec=grid_spec,
        out_shape=jax.ShapeDtypeStruct((n_sorted, D_MODEL), F32),
        compiler_params=_cparams(("arbitrary",)),
        name="moe_experts",
    )(tile_e, tile_nv, hs, w1, w3, w2)


def _combine_body(pos_hbm, x_ref, rf_ref, fg_ref, y_hbm, o_ref, idx, b1, b2, sem_i, sem_r,
                  *, final_norm):
    slot = _step_indices(pos_hbm, idx, sem_i)
    half_groups = TMC // DMA_GROUP // 2

    def copies(g, u, sem):
        t = g * DMA_GROUP + u
        return (pltpu.make_async_copy(y_hbm.at[pl.ds(idx[slot + t], 1), :],
                                      b1.at[g, pl.ds(u, 1), :], sem),
                pltpu.make_async_copy(y_hbm.at[pl.ds(idx[slot + TMC + t], 1), :],
                                      b2.at[g, pl.ds(u, 1), :], sem))

    def walk(half, start):
        def body(g, c):
            for u in range(DMA_GROUP):
                for cp in copies(half * half_groups + g, u, sem_r.at[half]):
                    cp.start() if start else cp.wait()
            return c
        lax.fori_loop(0, half_groups, body, 0)

    walk(0, True)
    walk(1, True)
    for half in range(2):
        walk(half, False)
        rows = pl.ds(half * (TMC // 2), TMC // 2)
        groups = pl.ds(half * half_groups, half_groups)
        g1 = rf_ref[rows, 0:1]
        g2 = rf_ref[rows, 1:2]
        y = x_ref[rows, :] + (g1 * b1[groups].reshape(TMC // 2, D_MODEL)
                              + g2 * b2[groups].reshape(TMC // 2, D_MODEL))
        if final_norm:
            y = _rms(y, fg_ref[...])
        o_ref[rows, :] = y


def _combine(pos, x2, rf, fg, ys, final_norm):
    t = x2.shape[0]
    return pl.pallas_call(
        functools.partial(_combine_body, final_norm=final_norm),
        grid=(t // TMC,),
        in_specs=[pl.BlockSpec(memory_space=pl.ANY),
                  pl.BlockSpec((TMC, D_MODEL), lambda i: (i, 0)),
                  pl.BlockSpec((TMC, LANES), lambda i: (i, 0)),
                  pl.BlockSpec((1, D_MODEL), lambda i: (0, 0)),
                  pl.BlockSpec(memory_space=pl.ANY)],
        out_specs=pl.BlockSpec((TMC, D_MODEL), lambda i: (i, 0)),
        out_shape=jax.ShapeDtypeStruct((t, D_MODEL), F32),
        scratch_shapes=[pltpu.SMEM((4 * TMC,), jnp.int32),
                        pltpu.VMEM((TMC // DMA_GROUP, DMA_GROUP, D_MODEL), F32),
                        pltpu.VMEM((TMC // DMA_GROUP, DMA_GROUP, D_MODEL), F32),
                        pltpu.SemaphoreType.DMA((2,)), pltpu.SemaphoreType.DMA((2,))],
        compiler_params=_cparams(("arbitrary",)),
        name="moe_combine",
    )(pos, x2, rf, fg, ys)


def _moe(x2d, att, ret, wo, ag, hm, g, router_pad, w1, w3, w2, fg, final_norm):
    t = x2d.shape[0]
    n_sorted = 2 * t + N_EXPERTS * TME
    n_tiles = n_sorted // TME
    x2, h, ri, rf, cnt = _post_route(x2d, att, ret, wo, ag, hm, g, router_pad)

    counts = cnt[0, :N_EXPERTS].astype(jnp.int32)
    padded = ((counts + TME - 1) // TME) * TME
    ends = jnp.cumsum(padded)
    starts = ends - padded
    experts = jnp.arange(N_EXPERTS, dtype=jnp.int32)
    start_of = lambda e: jnp.sum(jnp.where(e[:, None] == experts[None, :], starts[None, :], 0), axis=1)
    pos1 = start_of(ri[:, 0]) + ri[:, 2]
    pos2 = start_of(ri[:, 1]) + ri[:, 3]
    pos = jnp.concatenate([pos1.reshape(t // TMC, TMC), pos2.reshape(t // TMC, TMC)], axis=1)
    tile_start = jnp.arange(n_tiles, dtype=jnp.int32) * TME
    tile_e = jnp.minimum(jnp.sum(tile_start[:, None] >= ends[None, :], axis=1), N_EXPERTS - 1)
    tile_e = tile_e.astype(jnp.int32)
    tile_nv = jnp.clip(starts[tile_e] + counts[tile_e] - tile_start, 0, TME)
    tile_nv = jnp.where(tile_start < ends[-1], tile_nv, 0).astype(jnp.int32)
    ginfo = jnp.concatenate([ends, padded]).astype(jnp.int32)

    hs = _scatter(ginfo, pos, h, n_sorted)
    ys = _experts(tile_e, tile_nv, hs, w1, w3, w2)
    return _combine(pos, x2, rf, fg, ys, final_norm)


def _rope_tables(seq):
    pos = jnp.arange(seq, dtype=F32)[:, None]
    lane = jnp.arange(LANES)
    j = lane % ATTN_HEAD_DIM
    half = ROPE_DIM // 2
    inv_a = ROPE_THETA ** (-jnp.arange(half, dtype=F32) * 2.0 / ROPE_DIM)
    ang = pos * inv_a[None, :]
    cos_a, sin_a = jnp.cos(ang), jnp.sin(ang)
    jc = jnp.where(j < half, j, j - half) % half
    ca = jnp.where(j[None, :] < ROPE_DIM, cos_a[:, jc], 1.0)
    sa1 = jnp.where(((j >= half) & (j < ROPE_DIM))[None, :], sin_a[:, jc], 0.0)
    sa2 = jnp.where((j < half)[None, :], -sin_a[:, jc], 0.0)
    half_r = RET_HEAD_DIM // 2
    inv_r = RET_THETA ** (-jnp.arange(half_r, dtype=F32) * 2.0 / RET_HEAD_DIM)
    ang_r = pos * inv_r[None, :]
    cos_r, sin_r = jnp.cos(ang_r), jnp.sin(ang_r)
    cr = cos_r[:, lane % half_r]
    sr = jnp.where((lane < half_r)[None, :], -sin_r[:, lane % half_r], sin_r[:, lane % half_r])
    return ca, sa1, sa2, cr, sr


def _pack_router(router):
    hi = router.astype(BF16)
    lo = (router - hi.astype(F32)).astype(BF16)
    packed = jnp.concatenate([hi, lo], axis=-1)
    return jnp.pad(packed, ((0, 0), (0, 0), (0, LANES - 2 * N_EXPERTS)))


def _trunk(x, w):
    batch, seq, _ = x.shape
    t = batch * seq
    assert seq % (16 * ATT_BQ) == 0 and seq % TM == 0 and t % TMC == 0
    tabs = _rope_tables(seq)
    x2d = x.reshape(t, D_MODEL)
    depth = w["w_in"].shape[0]
    head_of = jnp.arange(ATTN_WIDTH) // ATTN_HEAD_DIM
    head_mean = jnp.where(head_of[:, None] == head_of[None, :], 1.0 / ATTN_HEAD_DIM, 0.0).astype(BF16)
    for layer in range(depth):
        last = layer == depth - 1
        ag = w["attn_out_g"][layer][None, :]
        proj = _in_proj(x2d, w["attn_norm_g"][layer][None, :], w["w_in"][layer], tabs, seq)
        att = _attention(proj, batch, seq)
        ret = _retention(proj, w["ret_out_g"][layer].reshape(N_RET_HEADS, 1, LANES),
                         w["lg"][layer], batch, seq)
        i = layer // 2
        if layer % 2 == 0:
            x2d = _post_dense(x2d, att, ret, w["w_out"][layer], ag, head_mean,
                              w["ffn_norm_g"][layer][None, :],
                              w["dense_w1"][i], w["dense_w3"][i], w["dense_w2"][i],
                              w["final_norm_g"], last)
        else:
            x2d = _moe(x2d, att, ret, w["w_out"][layer], ag, head_mean,
                       w["ffn_norm_g"][layer][None, :],
                       w["router_pad"][i], w["moe_w1"][i], w["moe_w3"][i], w["moe_w2"][i],
                       w["final_norm_g"], last)
    return x2d.reshape(batch, seq, D_MODEL)


def kernel(x_prompt, x_sample, attn_norm_g, w_in, attn_out_g, ret_out_g, ret_decay_fwd, ret_decay_bwd, w_out, ffn_norm_g, dense_w1, dense_w3, dense_w2, moe_router, moe_w1, moe_w3, moe_w2, final_norm_g):
    lg_f = jnp.log1p(-jnp.exp(ret_decay_fwd.astype(F32)))
    lg_b = jnp.log1p(-jnp.exp(ret_decay_bwd.astype(F32)))
    w = {
        "attn_norm_g": attn_norm_g, "attn_out_g": attn_out_g, "ret_out_g": ret_out_g,
        "ffn_norm_g": ffn_norm_g, "final_norm_g": final_norm_g[None, :],
        "lg": jnp.stack([lg_f, lg_b], axis=1),
        "w_in": w_in.astype(BF16), "w_out": w_out.astype(BF16),
        "dense_w1": dense_w1.astype(BF16), "dense_w3": dense_w3.astype(BF16),
        "dense_w2": dense_w2.astype(BF16),
        "moe_w1": moe_w1.astype(BF16), "moe_w3": moe_w3.astype(BF16), "moe_w2": moe_w2.astype(BF16),
        "router_pad": _pack_router(moe_router.astype(F32)),
    }
    return _trunk(x_prompt, w), _trunk(x_sample, w)
```

```python
import functools

import jax
import jax.numpy as jnp
from jax import lax
from jax.experimental import pallas as pl
from jax.experimental.pallas import tpu as pltpu

F32 = jnp.float32
BF16 = jnp.bfloat16

D_MODEL = 1024
N_ATTN_HEADS = 8
ATTN_HEAD_DIM = 64
ATTN_WIDTH = N_ATTN_HEADS * ATTN_HEAD_DIM
ATTN_RADIUS = 64
ROPE_THETA = 500000.0
ROPE_DIM = ATTN_HEAD_DIM // 4
N_RET_HEADS = 4
RET_HEAD_DIM = 128
RET_WIDTH = N_RET_HEADS * RET_HEAD_DIM
RET_THETA = 10000.0
RET_CHUNK = 128
IN_COLS = 3 * ATTN_WIDTH + 4 * RET_WIDTH
N_EXPERTS = 8
D_FF = 2816
NORM_EPS = 1e-6
NEG_INF = -1e30
TINY = 1e-30

LANES = 128
N_COL_BLOCKS = IN_COLS // LANES
VMEM_LIMIT = 56 * 1024 * 1024
ATTN_VMEM_LIMIT = 58 * 1024 * 1024

TM = 512
TME = 512
TMC = 1024
FF_CHUNKS = ((0, 1024), (1024, 2048), (2048, 2816))
ATT_BQ = 128
ATT_W = ATT_BQ + 2 * ATTN_RADIUS
ROUTE_PARTS = 2
ATT_NB = 8
RET_UNROLL = 16
DMA_GROUP = 8


def _cparams(sem, vmem_limit=VMEM_LIMIT):
    return pltpu.CompilerParams(dimension_semantics=sem, vmem_limit_bytes=vmem_limit)


def _rms(x, g):
    return x * lax.rsqrt(jnp.mean(x * x, axis=-1, keepdims=True) + NORM_EPS) * g


def _silu(x):
    return x / (1.0 + jnp.exp(-x))


def _in_proj_body(x_ref, g_ref, w_ref, ca_ref, sa1_ref, sa2_ref, cr_ref, sr_ref, o_ref):
    h = _rms(x_ref[...], g_ref[...]).astype(BF16)
    for c in range(7):
        acc = jnp.dot(h, w_ref[:, c * 512:(c + 1) * 512], preferred_element_type=F32)
        for s in range(4):
            a = acc[:, s * LANES:(s + 1) * LANES]
            if c in (0, 1):
                a = (a * ca_ref[...] + pltpu.roll(a, ROPE_DIM // 2, 1) * sa1_ref[...]
                     + pltpu.roll(a, LANES - ROPE_DIM // 2, 1) * sa2_ref[...])
                if c == 0:
                    a = a * (ATTN_HEAD_DIM ** -0.5)
            elif c in (3, 4):
                a = a * cr_ref[...] + pltpu.roll(a, RET_HEAD_DIM // 2, 1) * sr_ref[...]
                if c == 4:
                    a = a * (RET_HEAD_DIM ** -0.5)
            o_ref[c * 4 + s] = a.astype(BF16)


def _in_proj(x2d, g, w, tabs, seq):
    t = x2d.shape[0]
    nblk = seq // TM
    tab_spec = pl.BlockSpec((TM, LANES), lambda i: (i % nblk, 0))
    return pl.pallas_call(
        _in_proj_body,
        grid=(t // TM,),
        in_specs=[pl.BlockSpec((TM, D_MODEL), lambda i: (i, 0)),
                  pl.BlockSpec((1, D_MODEL), lambda i: (0, 0)),
                  pl.BlockSpec((D_MODEL, IN_COLS), lambda i: (0, 0)),
                  tab_spec, tab_spec, tab_spec, tab_spec, tab_spec],
        out_specs=pl.BlockSpec((N_COL_BLOCKS, TM, LANES), lambda i: (0, i, 0)),
        out_shape=jax.ShapeDtypeStruct((N_COL_BLOCKS, t, LANES), BF16),
        compiler_params=_cparams(("arbitrary",)),
        name="in_proj",
    )(x2d, g, w, *tabs)


def _attn_body(q_ref, k_ref, v_ref, o_ref,
               k1, v1, q4, k4, v4, q16, k16, v16, bias,
               acc_a, ms_a, ls_a, acc_b, ms_b, ls_b, *, seq):
    st_a = (acc_a, ms_a, ls_a)
    st_b = (acc_b, ms_b, ls_b)
    stage = 512
    pad = ATTN_RADIUS
    lane = lax.broadcasted_iota(jnp.int32, (ATT_BQ, LANES), 1)
    head0 = lane < ATTN_HEAD_DIM
    khead0 = lax.broadcasted_iota(jnp.int32, (ATT_W, LANES), 1) < ATTN_HEAD_DIM

    qi = lax.broadcasted_iota(jnp.int32, (ATT_BQ, ATT_W), 0)
    kj = lax.broadcasted_iota(jnp.int32, (ATT_BQ, ATT_W), 1)
    band = jnp.abs(kj - pad - qi) <= ATTN_RADIUS
    for e in range(4):
        ok = band
        if e & 1:
            ok = ok & (kj >= pad)
        if e & 2:
            ok = ok & (kj < ATT_BQ + pad)
        bias[e] = jnp.where(ok, 0.0, NEG_INF)

    for arr in (k1, v1, k4, v4, k16, v16):
        arr[pl.ds(0, pad), :] = jnp.zeros((pad, LANES), BF16)
        arr[pl.ds(seq + pad, pad), :] = jnp.zeros((pad, LANES), BF16)

    cls4, cls16 = seq // 4, seq // 16
    xf, x4f = st_a[0], st_b[0]
    for src, d1, d4, d16 in ((q_ref, None, q4, q16), (k_ref, k1, k4, k16), (v_ref, v1, v4, v16)):
        off = 0 if d1 is None else pad

        def upcast(i, c, src=src, d1=d1):
            r0 = pl.multiple_of(i * stage, stage)
            x = src[pl.ds(r0, stage), :]
            xf[pl.ds(r0, stage), :] = x.astype(F32)
            if d1 is not None:
                d1[pl.ds(pad + r0, stage), :] = x
            return c
        lax.fori_loop(0, seq // stage, upcast, 0)
        ch = min(cls4, 256)
        for a in range(4):
            def to4(i, c, d4=d4, a=a, ch=ch, off=off):
                j0 = pl.multiple_of(i * ch, ch)
                x = xf[pl.ds(a + 4 * j0, ch, stride=4), :]
                x4f[pl.ds(a * cls4 + j0, ch), :] = x
                d4[pl.ds(off + a * cls4 + j0, ch), :] = x.astype(BF16)
                return c
            lax.fori_loop(0, cls4 // ch, to4, 0)
        ch = min(cls16, 256)
        for a in range(4):
            for b in range(4):
                def to16(i, c, d16=d16, a=a, b=b, ch=ch, off=off):
                    i0 = pl.multiple_of(i * ch, ch)
                    d16[pl.ds(off + a * cls4 + b * cls16 + i0, ch), :] = (
                        x4f[pl.ds(a * cls4 + b + 4 * i0, ch, stride=4), :].astype(BF16))
                    return c
                lax.fori_loop(0, cls16 // ch, to16, 0)

    def relayout_16_to_4(src, dst):
        ch = min(cls16, ATT_BQ)
        for a in range(4):
            for b in range(4):
                def body(i, c, a=a, b=b):
                    i0 = pl.multiple_of(i * ch, ch)
                    for s_, d_ in zip(src, dst):
                        d_[pl.ds(a * cls4 + b + 4 * i0, ch, stride=4), :] = (
                            s_[pl.ds(a * cls4 + b * cls16 + i0, ch), :])
                    return c
                lax.fori_loop(0, cls16 // ch, body, 0)

    def relayout_4_to_1(src, dst):
        ch = ATT_BQ
        for a in range(4):
            def body(i, c, a=a):
                j0 = pl.multiple_of(i * ch, ch)
                for s_, d_ in zip(src, dst):
                    d_[pl.ds(a + 4 * j0, ch, stride=4), :] = s_[pl.ds(a * cls4 + j0, ch), :]
                return c
            lax.fori_loop(0, cls4 // ch, body, 0)

    def branch(d, qd, kd, vd, state, mode):
        acc, ms, ls = state
        nb = ATT_NB
        nblk = seq // d // ATT_BQ

        def group(g, c):
            blocks = []
            for u in range(nb):
                t = g * nb + u
                base = pl.multiple_of(t * ATT_BQ, ATT_BQ)
                n = t & (nblk - 1)
                edge = jnp.where(n == 0, 1, 0) + jnp.where(n == nblk - 1, 2, 0)
                blocks.append((base, edge))
            scores = []
            for base, edge in blocks:
                qb = qd[pl.ds(base, ATT_BQ), :]
                kw = kd[pl.ds(base, ATT_W), :]
                for hd in range(2):
                    qh = jnp.where(head0 if hd == 0 else ~head0, qb, jnp.zeros_like(qb))
                    scores.append(lax.dot_general(qh, kw, (((1,), (1,)), ((), ())),
                                                  preferred_element_type=F32) + bias[edge])
            maxes = [jnp.max(s, axis=1, keepdims=True) for s in scores]
            probs = [jnp.exp(s - m).astype(BF16) for s, m in zip(scores, maxes)]
            fulls = []
            for i, (base, edge) in enumerate(blocks):
                vw = vd[pl.ds(base, ATT_W), :]
                for hd in range(2):
                    vh = jnp.where(khead0 if hd == 0 else ~khead0, vw, jnp.ones_like(vw))
                    fulls.append(jnp.dot(probs[2 * i + hd], vh, preferred_element_type=F32))
            for i, (base, edge) in enumerate(blocks):
                f0, f1 = fulls[2 * i], fulls[2 * i + 1]
                o_new = jnp.where(head0, f0, f1)
                l_new = pltpu.roll(jnp.where(head0, f1, f0), ATTN_HEAD_DIM, 1)
                l_new = jnp.maximum(l_new, TINY)
                m_new = jnp.where(head0, maxes[2 * i], maxes[2 * i + 1])
                rows = pl.ds(base, ATT_BQ)
                if mode == "first":
                    acc[rows, :] = o_new
                    ms[rows, :] = m_new
                    ls[rows, :] = l_new
                    continue
                m_old = ms[rows, :]
                m_tot = jnp.maximum(m_old, m_new)
                a_old = jnp.exp(m_old - m_tot)
                a_new = jnp.exp(m_new - m_tot)
                o_tot = acc[rows, :] * a_old + o_new * a_new
                l_tot = ls[rows, :] * a_old + l_new * a_new
                if mode == "merge":
                    acc[rows, :] = o_tot
                    ls[rows, :] = l_tot
                    ms[rows, :] = m_tot
                else:
                    o_ref[rows, :] = (o_tot / l_tot).astype(BF16)
            return c

        lax.fori_loop(0, seq // (ATT_BQ * nb), group, 0)

    branch(16, q16, k16, v16, st_a, "first")
    relayout_16_to_4(st_a, st_b)
    branch(4, q4, k4, v4, st_b, "merge")
    relayout_4_to_1(st_b, st_a)
    branch(1, q_ref, k1, v1, st_a, "last")


def _attention(proj, batch, seq):
    t = batch * seq
    padded = seq + 2 * ATTN_RADIUS
    qkv_spec = lambda off, mode=None: pl.BlockSpec(
        (None, seq, LANES), lambda b, hp: (off + hp, b, 0), pipeline_mode=mode)
    return pl.pallas_call(
        functools.partial(_attn_body, seq=seq),
        grid=(batch, ATTN_WIDTH // LANES),
        in_specs=[qkv_spec(0), qkv_spec(4), qkv_spec(8, pl.Buffered(1))],
        out_specs=pl.BlockSpec((seq, LANES), lambda b, hp: (b, hp)),
        out_shape=jax.ShapeDtypeStruct((t, ATTN_WIDTH), BF16),
        scratch_shapes=[pltpu.VMEM((padded, LANES), BF16), pltpu.VMEM((padded, LANES), BF16),
                        pltpu.VMEM((seq, LANES), BF16),
                        pltpu.VMEM((padded, LANES), BF16), pltpu.VMEM((padded, LANES), BF16),
                        pltpu.VMEM((seq, LANES), BF16),
                        pltpu.VMEM((padded, LANES), BF16), pltpu.VMEM((padded, LANES), BF16),
                        pltpu.VMEM((4, ATT_BQ, ATT_W), F32)]
        + [pltpu.VMEM((seq, LANES), F32)] * 6,
        compiler_params=_cparams(("arbitrary", "arbitrary"), ATTN_VMEM_LIMIT),
        name="dilated_attn",
    )(proj, proj, proj)


def _ret_body(lg_ref, q_ref, k_ref, v_ref, gate_ref, gain_ref, o_ref, ob, dec, *, seq):
    c_len = RET_CHUNK
    n_chunks = seq // c_len
    hd = pl.program_id(1)
    lf = lg_ref[0, hd]
    lb = lg_ref[1, hd]
    ri = lax.broadcasted_iota(jnp.int32, (c_len, c_len), 0).astype(F32)
    ci = lax.broadcasted_iota(jnp.int32, (c_len, c_len), 1).astype(F32)
    diff = ri - ci
    dec[0] = jnp.where(diff >= 0, jnp.exp(lf * jnp.maximum(diff, 0.0)),
                       jnp.exp(lb * jnp.maximum(-diff, 0.0)))
    dec[1] = jnp.exp(lf * (ri + 1.0))
    dec[2] = jnp.exp(lf * (c_len - 1.0 - ri))
    dec[3] = jnp.exp(lb * (c_len - ri))
    dec[4] = jnp.exp(lb * ri)
    cdf = jnp.exp(lf * c_len)
    cdb = jnp.exp(lb * c_len)
    tdot = (((0,), (0,)), ((), ()))

    def bwd(jg, state):
        row_sets = [pl.ds(pl.multiple_of((n_chunks - 1 - (jg * RET_UNROLL + u)) * c_len, c_len), c_len)
                    for u in range(RET_UNROLL)]
        states = []
        for rows in row_sets:
            states.append(state)
            kn = k_ref[rows, :].astype(F32)
            kv = lax.dot_general((kn * dec[4]).astype(BF16), v_ref[rows, :], tdot,
                                 preferred_element_type=F32)
            state = state * cdb + kv
        for rows, st in zip(row_sets, states):
            qn = q_ref[rows, :].astype(F32)
            ob[rows, :] = jnp.dot((qn * dec[3]).astype(BF16), st.astype(BF16),
                                  preferred_element_type=F32)
        return state

    lax.fori_loop(0, n_chunks // RET_UNROLL, bwd, jnp.zeros((c_len, c_len), F32))

    gain = gain_ref[...]

    def fwd(jg, state):
        row_sets = [pl.ds(pl.multiple_of((jg * RET_UNROLL + u) * c_len, c_len), c_len)
                    for u in range(RET_UNROLL)]
        states = []
        for rows in row_sets:
            states.append(state)
            kn = k_ref[rows, :].astype(F32)
            kv = lax.dot_general((kn * dec[2]).astype(BF16), v_ref[rows, :], tdot,
                                 preferred_element_type=F32)
            state = state * cdf + kv
        for rows, st in zip(row_sets, states):
            qb = q_ref[rows, :]
            vb = v_ref[rows, :]
            sc = lax.dot_general(qb, k_ref[rows, :], (((1,), (1,)), ((), ())),
                                 preferred_element_type=F32) * dec[0]
            tot = (ob[rows, :] + jnp.dot(sc.astype(BF16), vb, preferred_element_type=F32)
                   + jnp.dot((qb.astype(F32) * dec[1]).astype(BF16), st.astype(BF16),
                             preferred_element_type=F32))
            y = tot - jnp.mean(tot, axis=-1, keepdims=True)
            y = y * lax.rsqrt(jnp.mean(y * y, axis=-1, keepdims=True) + NORM_EPS) * gain
            o_ref[rows, :] = (y * _silu(gate_ref[rows, :].astype(F32))).astype(BF16)
        return state

    lax.fori_loop(0, n_chunks // RET_UNROLL, fwd, jnp.zeros((c_len, c_len), F32))


def _retention(proj, gain, lg, batch, seq):
    t = batch * seq
    spec = lambda off: pl.BlockSpec((None, seq, LANES), lambda b, hd: (off + hd, b, 0))
    return pl.pallas_call(
        functools.partial(_ret_body, seq=seq),
        grid=(batch, N_RET_HEADS),
        in_specs=[pl.BlockSpec(memory_space=pltpu.SMEM),
                  spec(12), spec(16), spec(20), spec(24),
                  pl.BlockSpec((None, 1, LANES), lambda b, hd: (hd, 0, 0))],
        out_specs=pl.BlockSpec((seq, LANES), lambda b, hd: (b, hd)),
        out_shape=jax.ShapeDtypeStruct((t, RET_WIDTH), BF16),
        scratch_shapes=[pltpu.VMEM((seq, LANES), F32),
                        pltpu.VMEM((5, RET_CHUNK, RET_CHUNK), F32)],
        compiler_params=_cparams(("arbitrary", "arbitrary")),
        name="retention",
    )(lg, proj, proj, proj, proj, gain)


def _out_proj(x_ref, a_ref, r_ref, wo_ref, ag_ref, hm_ref, rows=slice(None)):
    a = a_ref[rows, :].astype(F32)
    mean = jnp.dot((a * a).astype(BF16), hm_ref[...], preferred_element_type=F32)
    an = (a * lax.rsqrt(mean + NORM_EPS) * ag_ref[...]).astype(BF16)
    return (x_ref[rows, :]
            + jnp.dot(an, wo_ref[0:ATTN_WIDTH, :], preferred_element_type=F32)
            + jnp.dot(r_ref[rows, :], wo_ref[ATTN_WIDTH:, :], preferred_element_type=F32))


def _post_dense_body(x_ref, a_ref, r_ref, wo_ref, ag_ref, hm_ref, g_ref, w1_ref, w3_ref, w2_ref,
                     fg_ref, o_ref, *, final_norm):
    x2 = _out_proj(x_ref, a_ref, r_ref, wo_ref, ag_ref, hm_ref)
    h = _rms(x2, g_ref[...]).astype(BF16)
    ffn = None
    for c0, c1 in FF_CHUNKS:
        a = jnp.dot(h, w1_ref[:, c0:c1], preferred_element_type=F32)
        b = jnp.dot(h, w3_ref[:, c0:c1], preferred_element_type=F32)
        u = (_silu(a) * b).astype(BF16)
        d = jnp.dot(u, w2_ref[c0:c1, :], preferred_element_type=F32)
        ffn = d if ffn is None else ffn + d
    y = x2 + ffn
    if final_norm:
        y = _rms(y, fg_ref[...])
    o_ref[...] = y


def _post_dense(x2d, att, ret, wo, ag, hm, g, w1, w3, w2, fg, final_norm):
    t = x2d.shape[0]
    full = lambda a: pl.BlockSpec(a.shape, lambda i: (0,) * a.ndim)
    return pl.pallas_call(
        functools.partial(_post_dense_body, final_norm=final_norm),
        grid=(t // TM,),
        in_specs=[pl.BlockSpec((TM, D_MODEL), lambda i: (i, 0)),
                  pl.BlockSpec((TM, ATTN_WIDTH), lambda i: (i, 0)),
                  pl.BlockSpec((TM, RET_WIDTH), lambda i: (i, 0)),
                  full(wo), full(ag), full(hm), full(g), full(w1), full(w3), full(w2), full(fg)],
        out_specs=pl.BlockSpec((TM, D_MODEL), lambda i: (i, 0)),
        out_shape=jax.ShapeDtypeStruct((t, D_MODEL), F32),
        compiler_params=_cparams(("arbitrary",)),
        name="post_dense",
    )(x2d, att, ret, wo, ag, hm, g, w1, w3, w2, fg)


def _post_route_body(x_ref, a_ref, r_ref, wo_ref, ag_ref, hm_ref, g_ref, rt_ref,
                     x2_ref, h_ref, ri_ref, rf_ref, cnt_ref, base):
    @pl.when(pl.program_id(0) == 0)
    def _():
        base[...] = jnp.zeros_like(base)

    part = TM // ROUTE_PARTS
    parts = [pl.ds(k * part, part) for k in range(ROUTE_PARTS)]
    x2s = [_out_proj(x_ref, a_ref, r_ref, wo_ref, ag_ref, hm_ref, rows) for rows in parts]
    hs = [_rms(x2, g_ref[...]) for x2 in x2s]
    for rows, x2, h in zip(parts, x2s, hs):
        x2_ref[rows, :] = x2
        h_ref[rows, :] = h
    logits = []
    for h in hs:
        hi = h.astype(BF16)
        lo = (h - hi.astype(F32)).astype(BF16)
        pieces = (jnp.dot(hi, rt_ref[...], preferred_element_type=F32)
                  + jnp.dot(lo, rt_ref[...], preferred_element_type=F32))
        logits.append(pieces + pltpu.roll(pieces, LANES - N_EXPERTS, 1))
    lane = lax.broadcasted_iota(jnp.int32, (part, LANES), 1).astype(F32)
    row = lax.broadcasted_iota(jnp.int32, (part, part), 0)
    col = lax.broadcasted_iota(jnp.int32, (part, part), 1)
    earlier = jnp.where(col < row, 1.0, 0.0).astype(BF16)
    routed = []
    for lgt in logits:
        lg = jnp.where(lane < N_EXPERTS, lgt, -jnp.inf)
        m1 = jnp.max(lg, axis=1, keepdims=True)
        i1 = jnp.min(jnp.where(lg == m1, lane, float(LANES)), axis=1, keepdims=True)
        lg2 = jnp.where(lane == i1, -jnp.inf, lg)
        m2 = jnp.max(lg2, axis=1, keepdims=True)
        i2 = jnp.min(jnp.where(lg2 == m2, lane, float(LANES)), axis=1, keepdims=True)
        e = jnp.exp(m2 - m1)
        sel1 = lane == i1
        sel2 = lane == i2
        onehot = jnp.where(sel1 | sel2, 1.0, 0.0)
        within = jnp.dot(earlier, onehot.astype(BF16), preferred_element_type=F32)
        routed.append((i1, i2, 1.0 / (1.0 + e), e / (1.0 + e), sel1, sel2, onehot, within))
    running = base[...]
    for rows, (i1, i2, g1, g2, sel1, sel2, onehot, within) in zip(parts, routed):
        cnt = within + running
        rank1 = jnp.sum(jnp.where(sel1, cnt, 0.0), axis=1, keepdims=True)
        rank2 = jnp.sum(jnp.where(sel2, cnt, 0.0), axis=1, keepdims=True)
        running = running + jnp.sum(onehot, axis=0, keepdims=True)
        ri = jnp.where(lane == 0, i1, jnp.where(lane == 1, i2,
                       jnp.where(lane == 2, rank1, jnp.where(lane == 3, rank2, 0.0))))
        ri_ref[:, rows] = jnp.transpose(ri)[0:8, :].astype(jnp.int32)
        rf_ref[rows, :] = jnp.where(lane == 0, g1, jnp.where(lane == 1, g2, 0.0))
    base[...] = running
    cnt_ref[...] = jnp.broadcast_to(running, cnt_ref.shape)


def _post_route(x2d, att, ret, wo, ag, hm, g, router_pad):
    t = x2d.shape[0]
    full = lambda a: pl.BlockSpec(a.shape, lambda i: (0,) * a.ndim)
    row = lambda w: pl.BlockSpec((TM, w), lambda i: (i, 0))
    return pl.pallas_call(
        _post_route_body,
        grid=(t // TM,),
        in_specs=[row(D_MODEL), row(ATTN_WIDTH), row(RET_WIDTH),
                  full(wo), full(ag), full(hm), full(g), full(router_pad)],
        out_specs=[row(D_MODEL), row(D_MODEL), pl.BlockSpec((8, TM), lambda i: (0, i)), row(LANES),
                   pl.BlockSpec((8, LANES), lambda i: (0, 0))],
        out_shape=[jax.ShapeDtypeStruct((t, D_MODEL), F32),
                   jax.ShapeDtypeStruct((t, D_MODEL), F32),
                   jax.ShapeDtypeStruct((8, t), jnp.int32),
                   jax.ShapeDtypeStruct((t, LANES), F32),
                   jax.ShapeDtypeStruct((8, LANES), F32)],
        scratch_shapes=[pltpu.VMEM((1, LANES), F32)],
        compiler_params=_cparams(("arbitrary",)),
        name="post_route",
    )(x2d, att, ret, wo, ag, hm, g, router_pad)


def _step_indices(pos_hbm, idx, sem_i):
    i = pl.program_id(0)
    slot = i % 2
    width = 2 * TMC

    def fetch(step, s):
        dst = idx.at[pl.ds(pl.multiple_of(s * width, width), width)]
        return pltpu.make_async_copy(pos_hbm.at[step], dst, sem_i.at[s])

    @pl.when(i == 0)
    def _():
        fetch(0, 0).start()

    fetch(i, slot).wait()

    @pl.when(i + 1 < pl.num_programs(0))
    def _():
        fetch(i + 1, 1 - slot).start()

    return pl.multiple_of(slot * width, width)


def _scatter_body(ginfo, pos_hbm, h_ref, o_hbm, idx, zbuf, sem_i, sem_r, sem_z, *, n_sorted):
    i = pl.program_id(0)

    @pl.when(i == 0)
    def _():
        zbuf[...] = jnp.zeros_like(zbuf)
        total = ginfo[N_EXPERTS - 1]
        fills = []
        for e in range(N_EXPERTS):
            start = pl.multiple_of(jnp.maximum(ginfo[e] - TME, 0), TME)
            fills.append((ginfo[N_EXPERTS + e] > 0, start))
        for j in range(N_EXPERTS):
            start = n_sorted - (j + 1) * TME
            fills.append((start >= total, start))
        for phase in ("start", "wait"):
            for cond, start in fills:
                @pl.when(cond)
                def _(start=start, phase=phase):
                    cp = pltpu.make_async_copy(zbuf, o_hbm.at[pl.ds(start, TME), :], sem_z)
                    cp.start() if phase == "start" else cp.wait()

    slot = _step_indices(pos_hbm, idx, sem_i)

    def copies(g, u):
        t = g * DMA_GROUP + u
        src = h_ref.at[g, pl.ds(u, 1), :]
        return (pltpu.make_async_copy(src, o_hbm.at[pl.ds(idx[slot + t], 1), :], sem_r),
                pltpu.make_async_copy(src, o_hbm.at[pl.ds(idx[slot + TMC + t], 1), :], sem_r))

    def issue(g, c):
        for u in range(DMA_GROUP):
            for cp in copies(g, u):
                cp.start()
        return c

    lax.fori_loop(0, TMC // DMA_GROUP, issue, 0)

    def drain(g, c):
        for u in range(DMA_GROUP):
            for cp in copies(g, u):
                cp.wait()
        return c

    lax.fori_loop(0, TMC // DMA_GROUP, drain, 0)


def _scatter(ginfo, pos, h, n_sorted):
    t = h.shape[0]
    return pl.pallas_call(
        functools.partial(_scatter_body, n_sorted=n_sorted),
        grid=(t // TMC,),
        in_specs=[pl.BlockSpec(memory_space=pltpu.SMEM),
                  pl.BlockSpec(memory_space=pl.ANY),
                  pl.BlockSpec((TMC // DMA_GROUP, DMA_GROUP, D_MODEL), lambda i: (i, 0, 0))],
        out_specs=pl.BlockSpec(memory_space=pl.ANY),
        out_shape=jax.ShapeDtypeStruct((n_sorted, D_MODEL), F32),
        scratch_shapes=[pltpu.SMEM((4 * TMC,), jnp.int32),
                        pltpu.VMEM((TME, D_MODEL), F32),
                        pltpu.SemaphoreType.DMA((2,)), pltpu.SemaphoreType.DMA,
                        pltpu.SemaphoreType.DMA],
        compiler_params=_cparams(("arbitrary",)),
        name="moe_dispatch",
    )(ginfo, pos, h.reshape(t // DMA_GROUP, DMA_GROUP, D_MODEL))


def _expert_body(te_ref, nv_ref, h_ref, w1_ref, w3_ref, w2_ref, o_ref):
    nv = nv_ref[pl.program_id(0)]

    @pl.when(nv == 0)
    def _():
        o_ref[...] = jnp.zeros_like(o_ref)

    @pl.when(nv > 0)
    def _():
        rows = lax.broadcasted_iota(jnp.int32, (TME, D_MODEL), 0)
        h = jnp.where(rows < nv, h_ref[...], 0.0).astype(BF16)
        ffn = None
        for c0, c1 in FF_CHUNKS:
            a = jnp.dot(h, w1_ref[:, c0:c1], preferred_element_type=F32)
            b = jnp.dot(h, w3_ref[:, c0:c1], preferred_element_type=F32)
            u = (_silu(a) * b).astype(BF16)
            d = jnp.dot(u, w2_ref[c0:c1, :], preferred_element_type=F32)
            ffn = d if ffn is None else ffn + d
        o_ref[...] = ffn


def _experts(tile_e, tile_nv, hs, w1, w3, w2):
    n_sorted = hs.shape[0]
    grid_spec = pltpu.PrefetchScalarGridSpec(
        num_scalar_prefetch=2,
        grid=(n_sorted // TME,),
        in_specs=[pl.BlockSpec((TME, D_MODEL), lambda j, te, nv: (j, 0)),
                  pl.BlockSpec((None, D_MODEL, D_FF), lambda j, te, nv: (te[j], 0, 0)),
                  pl.BlockSpec((None, D_MODEL, D_FF), lambda j, te, nv: (te[j], 0, 0)),
                  pl.BlockSpec((None, D_FF, D_MODEL), lambda j, te, nv: (te[j], 0, 0))],
        out_specs=pl.BlockSpec((TME, D_MODEL), lambda j, te, nv: (j, 0)),
    )
    return pl.pallas_call(
        _expert_body,
        grid_spec=grid_spec,
        out_shape=jax.ShapeDtypeStruct((n_sorted, D_MODEL), F32),
        compiler_params=_cparams(("arbitrary",)),
        name="moe_experts",
    )(tile_e, tile_nv, hs, w1, w3, w2)


def _combine_body(pos_hbm, x_ref, rf_ref, fg_ref, y_hbm, o_ref, idx, b1, b2, sem_i, sem_r,
                  *, final_norm):
    slot = _step_indices(pos_hbm, idx, sem_i)
    half_groups = TMC // DMA_GROUP // 2

    def copies(g, u, sem):
        t = g * DMA_GROUP + u
        return (pltpu.make_async_copy(y_hbm.at[pl.ds(idx[slot + t], 1), :],
                                      b1.at[g, pl.ds(u, 1), :], sem),
                pltpu.make_async_copy(y_hbm.at[pl.ds(idx[slot + TMC + t], 1), :],
                                      b2.at[g, pl.ds(u, 1), :], sem))

    def walk(half, start):
        def body(g, c):
            for u in range(DMA_GROUP):
                for cp in copies(half * half_groups + g, u, sem_r.at[half]):
                    cp.start() if start else cp.wait()
            return c
        lax.fori_loop(0, half_groups, body, 0)

    walk(0, True)
    walk(1, True)
    for half in range(2):
        walk(half, False)
        rows = pl.ds(half * (TMC // 2), TMC // 2)
        groups = pl.ds(half * half_groups, half_groups)
        g1 = rf_ref[rows, 0:1]
        g2 = rf_ref[rows, 1:2]
        y = x_ref[rows, :] + (g1 * b1[groups].reshape(TMC // 2, D_MODEL)
                              + g2 * b2[groups].reshape(TMC // 2, D_MODEL))
        if final_norm:
            y = _rms(y, fg_ref[...])
        o_ref[rows, :] = y


def _combine(pos, x2, rf, fg, ys, final_norm):
    t = x2.shape[0]
    return pl.pallas_call(
        functools.partial(_combine_body, final_norm=final_norm),
        grid=(t // TMC,),
        in_specs=[pl.BlockSpec(memory_space=pl.ANY),
                  pl.BlockSpec((TMC, D_MODEL), lambda i: (i, 0)),
                  pl.BlockSpec((TMC, LANES), lambda i: (i, 0)),
                  pl.BlockSpec((1, D_MODEL), lambda i: (0, 0)),
                  pl.BlockSpec(memory_space=pl.ANY)],
        out_specs=pl.BlockSpec((TMC, D_MODEL), lambda i: (i, 0)),
        out_shape=jax.ShapeDtypeStruct((t, D_MODEL), F32),
        scratch_shapes=[pltpu.SMEM((4 * TMC,), jnp.int32),
                        pltpu.VMEM((TMC // DMA_GROUP, DMA_GROUP, D_MODEL), F32),
                        pltpu.VMEM((TMC // DMA_GROUP, DMA_GROUP, D_MODEL), F32),
                        pltpu.SemaphoreType.DMA((2,)), pltpu.SemaphoreType.DMA((2,))],
        compiler_params=_cparams(("arbitrary",)),
        name="moe_combine",
    )(pos, x2, rf, fg, ys)


def _moe(x2d, att, ret, wo, ag, hm, g, router_pad, w1, w3, w2, fg, final_norm):
    t = x2d.shape[0]
    n_sorted = 2 * t + N_EXPERTS * TME
    n_tiles = n_sorted // TME
    x2, h, ri, rf, cnt = _post_route(x2d, att, ret, wo, ag, hm, g, router_pad)

    counts = cnt[0, :N_EXPERTS].astype(jnp.int32)
    padded = ((counts + TME - 1) // TME) * TME
    ends = jnp.cumsum(padded)
    starts = ends - padded
    experts = jnp.arange(N_EXPERTS, dtype=jnp.int32)
    start_of = lambda e: jnp.sum(jnp.where(e[:, None] == experts[None, :], starts[None, :], 0), axis=1)
    pos1 = start_of(ri[0]) + ri[2]
    pos2 = start_of(ri[1]) + ri[3]
    pos = jnp.concatenate([pos1.reshape(t // TMC, TMC), pos2.reshape(t // TMC, TMC)], axis=1)
    tile_start = jnp.arange(n_tiles, dtype=jnp.int32) * TME
    tile_e = jnp.minimum(jnp.sum(tile_start[:, None] >= ends[None, :], axis=1), N_EXPERTS - 1)
    tile_e = tile_e.astype(jnp.int32)
    tile_nv = jnp.clip(starts[tile_e] + counts[tile_e] - tile_start, 0, TME)
    tile_nv = jnp.where(tile_start < ends[-1], tile_nv, 0).astype(jnp.int32)
    ginfo = jnp.concatenate([ends, padded]).astype(jnp.int32)

    hs = _scatter(ginfo, pos, h, n_sorted)
    ys = _experts(tile_e, tile_nv, hs, w1, w3, w2)
    return _combine(pos, x2, rf, fg, ys, final_norm)


def _rope_tables(seq):
    pos = jnp.arange(seq, dtype=F32)[:, None]
    lane = jnp.arange(LANES)
    j = lane % ATTN_HEAD_DIM
    half = ROPE_DIM // 2
    inv_a = ROPE_THETA ** (-jnp.arange(half, dtype=F32) * 2.0 / ROPE_DIM)
    ang = pos * inv_a[None, :]
    cos_a, sin_a = jnp.cos(ang), jnp.sin(ang)
    jc = jnp.where(j < half, j, j - half) % half
    ca = jnp.where(j[None, :] < ROPE_DIM, cos_a[:, jc], 1.0)
    sa1 = jnp.where(((j >= half) & (j < ROPE_DIM))[None, :], sin_a[:, jc], 0.0)
    sa2 = jnp.where((j < half)[None, :], -sin_a[:, jc], 0.0)
    half_r = RET_HEAD_DIM // 2
    inv_r = RET_THETA ** (-jnp.arange(half_r, dtype=F32) * 2.0 / RET_HEAD_DIM)
    ang_r = pos * inv_r[None, :]
    cos_r, sin_r = jnp.cos(ang_r), jnp.sin(ang_r)
    cr = cos_r[:, lane % half_r]
    sr = jnp.where((lane < half_r)[None, :], -sin_r[:, lane % half_r], sin_r[:, lane % half_r])
    return ca, sa1, sa2, cr, sr


def _pack_router(router):
    hi = router.astype(BF16)
    lo = (router - hi.astype(F32)).astype(BF16)
    packed = jnp.concatenate([hi, lo], axis=-1)
    return jnp.pad(packed, ((0, 0), (0, 0), (0, LANES - 2 * N_EXPERTS)))


def _trunk(x, w):
    batch, seq, _ = x.shape
    t = batch * seq
    assert seq % (16 * ATT_BQ) == 0 and seq % TM == 0 and t % TMC == 0
    tabs = _rope_tables(seq)
    x2d = x.reshape(t, D_MODEL)
    depth = w["w_in"].shape[0]
    head_of = jnp.arange(ATTN_WIDTH) // ATTN_HEAD_DIM
    head_mean = jnp.where(head_of[:, None] == head_of[None, :], 1.0 / ATTN_HEAD_DIM, 0.0).astype(BF16)
    for layer in range(depth):
        last = layer == depth - 1
        ag = w["attn_out_g"][layer][None, :]
        proj = _in_proj(x2d, w["attn_norm_g"][layer][None, :], w["w_in"][layer], tabs, seq)
        att = _attention(proj, batch, seq)
        ret = _retention(proj, w["ret_out_g"][layer].reshape(N_RET_HEADS, 1, LANES),
                         w["lg"][layer], batch, seq)
        i = layer // 2
        if layer % 2 == 0:
            x2d = _post_dense(x2d, att, ret, w["w_out"][layer], ag, head_mean,
                              w["ffn_norm_g"][layer][None, :],
                              w["dense_w1"][i], w["dense_w3"][i], w["dense_w2"][i],
                              w["final_norm_g"], last)
        else:
            x2d = _moe(x2d, att, ret, w["w_out"][layer], ag, head_mean,
                       w["ffn_norm_g"][layer][None, :],
                       w["router_pad"][i], w["moe_w1"][i], w["moe_w3"][i], w["moe_w2"][i],
                       w["final_norm_g"], last)
    return x2d.reshape(batch, seq, D_MODEL)


def kernel(x_prompt, x_sample, attn_norm_g, w_in, attn_out_g, ret_out_g, ret_decay_fwd, ret_decay_bwd, w_out, ffn_norm_g, dense_w1, dense_w3, dense_w2, moe_router, moe_w1, moe_w3, moe_w2, final_norm_g):
    lg_f = jnp.log1p(-jnp.exp(ret_decay_fwd.astype(F32)))
    lg_b = jnp.log1p(-jnp.exp(ret_decay_bwd.astype(F32)))
    w = {
        "attn_norm_g": attn_norm_g, "attn_out_g": attn_out_g, "ret_out_g": ret_out_g,
        "ffn_norm_g": ffn_norm_g, "final_norm_g": final_norm_g[None, :],
        "lg": jnp.stack([lg_f, lg_b], axis=1),
        "w_in": w_in.astype(BF16), "w_out": w_out.astype(BF16),
        "dense_w1": dense_w1.astype(BF16), "dense_w3": dense_w3.astype(BF16),
        "dense_w2": dense_w2.astype(BF16),
        "moe_w1": moe_w1.astype(BF16), "moe_w3": moe_w3.astype(BF16), "moe_w2": moe_w2.astype(BF16),
        "router_pad": _pack_router(moe_router.astype(F32)),
    }
    return _trunk(x_prompt, w), _trunk(x_sample, w)
```

```python
import functools

import jax
import jax.numpy as jnp
from jax import lax
from jax.experimental import pallas as pl
from jax.experimental.pallas import tpu as pltpu

F32 = jnp.float32
BF16 = jnp.bfloat16

D_MODEL = 1024
N_ATTN_HEADS = 8
ATTN_HEAD_DIM = 64
ATTN_WIDTH = N_ATTN_HEADS * ATTN_HEAD_DIM
ATTN_RADIUS = 64
ROPE_THETA = 500000.0
ROPE_DIM = ATTN_HEAD_DIM // 4
N_RET_HEADS = 4
RET_HEAD_DIM = 128
RET_WIDTH = N_RET_HEADS * RET_HEAD_DIM
RET_THETA = 10000.0
RET_CHUNK = 128
IN_COLS = 3 * ATTN_WIDTH + 4 * RET_WIDTH
N_EXPERTS = 8
D_FF = 2816
NORM_EPS = 1e-6
NEG_INF = -1e30
TINY = 1e-30

LANES = 128
N_COL_BLOCKS = IN_COLS // LANES
VMEM_LIMIT = 56 * 1024 * 1024
ATTN_VMEM_LIMIT = 58 * 1024 * 1024

TM = 512
TMI = 1024
TME = 512
TMC = 1024
FF_CHUNKS = ((0, 1024), (1024, 2048), (2048, 2816))
ATT_BQ = 128
ATT_W = ATT_BQ + 2 * ATTN_RADIUS
ROUTE_PARTS = 2
ATT_NB = 8
RET_UNROLL = 32
DMA_GROUP = 8


def _cparams(sem, vmem_limit=VMEM_LIMIT):
    return pltpu.CompilerParams(dimension_semantics=sem, vmem_limit_bytes=vmem_limit)


def _rms(x, g):
    return x * lax.rsqrt(jnp.mean(x * x, axis=-1, keepdims=True) + NORM_EPS) * g


def _silu(x):
    return x / (1.0 + jnp.exp(-x))


def _in_proj_body(x_ref, g_ref, w_ref, ca_ref, sa1_ref, sa2_ref, cr_ref, sr_ref, o_ref):
    h = _rms(x_ref[...], g_ref[...]).astype(BF16)
    for c in range(7):
        acc = jnp.dot(h, w_ref[:, c * 512:(c + 1) * 512], preferred_element_type=F32)
        for s in range(4):
            a = acc[:, s * LANES:(s + 1) * LANES]
            if c in (0, 1):
                a = (a * ca_ref[...] + pltpu.roll(a, ROPE_DIM // 2, 1) * sa1_ref[...]
                     + pltpu.roll(a, LANES - ROPE_DIM // 2, 1) * sa2_ref[...])
                if c == 0:
                    a = a * (ATTN_HEAD_DIM ** -0.5)
            elif c in (3, 4):
                a = a * cr_ref[...] + pltpu.roll(a, RET_HEAD_DIM // 2, 1) * sr_ref[...]
                if c == 4:
                    a = a * (RET_HEAD_DIM ** -0.5)
            o_ref[c * 4 + s] = a.astype(BF16)


def _in_proj(x2d, g, w, tabs, seq):
    t = x2d.shape[0]
    nblk = seq // TMI
    tab_spec = pl.BlockSpec((TMI, LANES), lambda i: (i % nblk, 0))
    return pl.pallas_call(
        _in_proj_body,
        grid=(t // TMI,),
        in_specs=[pl.BlockSpec((TMI, D_MODEL), lambda i: (i, 0)),
                  pl.BlockSpec((1, D_MODEL), lambda i: (0, 0)),
                  pl.BlockSpec((D_MODEL, IN_COLS), lambda i: (0, 0)),
                  tab_spec, tab_spec, tab_spec, tab_spec, tab_spec],
        out_specs=pl.BlockSpec((N_COL_BLOCKS, TMI, LANES), lambda i: (0, i, 0)),
        out_shape=jax.ShapeDtypeStruct((N_COL_BLOCKS, t, LANES), BF16),
        compiler_params=_cparams(("arbitrary",)),
        name="in_proj",
    )(x2d, g, w, *tabs)


def _attn_body(q_ref, k_ref, v_ref, o_ref,
               k1, v1, q4, k4, v4, q16, k16, v16, bias,
               acc_a, ms_a, ls_a, acc_b, ms_b, ls_b, *, seq):
    st_a = (acc_a, ms_a, ls_a)
    st_b = (acc_b, ms_b, ls_b)
    stage = 512
    pad = ATTN_RADIUS
    lane = lax.broadcasted_iota(jnp.int32, (ATT_BQ, LANES), 1)
    head0 = lane < ATTN_HEAD_DIM
    khead0 = lax.broadcasted_iota(jnp.int32, (ATT_W, LANES), 1) < ATTN_HEAD_DIM

    qi = lax.broadcasted_iota(jnp.int32, (ATT_BQ, ATT_W), 0)
    kj = lax.broadcasted_iota(jnp.int32, (ATT_BQ, ATT_W), 1)
    band = jnp.abs(kj - pad - qi) <= ATTN_RADIUS
    for e in range(4):
        ok = band
        if e & 1:
            ok = ok & (kj >= pad)
        if e & 2:
            ok = ok & (kj < ATT_BQ + pad)
        bias[e] = jnp.where(ok, 0.0, NEG_INF)

    for arr in (k1, v1, k4, v4, k16, v16):
        arr[pl.ds(0, pad), :] = jnp.zeros((pad, LANES), BF16)
        arr[pl.ds(seq + pad, pad), :] = jnp.zeros((pad, LANES), BF16)

    cls4, cls16 = seq // 4, seq // 16
    xf, x4f = st_a[0], st_b[0]
    for src, d1, d4, d16 in ((q_ref, None, q4, q16), (k_ref, k1, k4, k16), (v_ref, v1, v4, v16)):
        off = 0 if d1 is None else pad

        def upcast(i, c, src=src, d1=d1):
            r0 = pl.multiple_of(i * stage, stage)
            x = src[pl.ds(r0, stage), :]
            xf[pl.ds(r0, stage), :] = x.astype(F32)
            if d1 is not None:
                d1[pl.ds(pad + r0, stage), :] = x
            return c
        lax.fori_loop(0, seq // stage, upcast, 0)
        ch = min(cls4, 256)
        for a in range(4):
            def to4(i, c, d4=d4, a=a, ch=ch, off=off):
                j0 = pl.multiple_of(i * ch, ch)
                x = xf[pl.ds(a + 4 * j0, ch, stride=4), :]
                x4f[pl.ds(a * cls4 + j0, ch), :] = x
                d4[pl.ds(off + a * cls4 + j0, ch), :] = x.astype(BF16)
                return c
            lax.fori_loop(0, cls4 // ch, to4, 0)
        ch = min(cls16, 256)
        for a in range(4):
            for b in range(4):
                def to16(i, c, d16=d16, a=a, b=b, ch=ch, off=off):
                    i0 = pl.multiple_of(i * ch, ch)
                    d16[pl.ds(off + a * cls4 + b * cls16 + i0, ch), :] = (
                        x4f[pl.ds(a * cls4 + b + 4 * i0, ch, stride=4), :].astype(BF16))
                    return c
                lax.fori_loop(0, cls16 // ch, to16, 0)

    def relayout_16_to_4(src, dst):
        ch = min(cls16, ATT_BQ)
        for a in range(4):
            for b in range(4):
                def body(i, c, a=a, b=b):
                    i0 = pl.multiple_of(i * ch, ch)
                    for s_, d_ in zip(src, dst):
                        d_[pl.ds(a * cls4 + b + 4 * i0, ch, stride=4), :] = (
                            s_[pl.ds(a * cls4 + b * cls16 + i0, ch), :])
                    return c
                lax.fori_loop(0, cls16 // ch, body, 0)

    def relayout_4_to_1(src, dst):
        ch = ATT_BQ
        for a in range(4):
            def body(i, c, a=a):
                j0 = pl.multiple_of(i * ch, ch)
                for s_, d_ in zip(src, dst):
                    d_[pl.ds(a + 4 * j0, ch, stride=4), :] = s_[pl.ds(a * cls4 + j0, ch), :]
                return c
            lax.fori_loop(0, cls4 // ch, body, 0)

    def branch(d, qd, kd, vd, state, mode):
        acc, ms, ls = state
        nb = ATT_NB
        nblk = seq // d // ATT_BQ

        def group(g, c):
            blocks = []
            for u in range(nb):
                t = g * nb + u
                base = pl.multiple_of(t * ATT_BQ, ATT_BQ)
                n = t & (nblk - 1)
                edge = jnp.where(n == 0, 1, 0) + jnp.where(n == nblk - 1, 2, 0)
                blocks.append((base, edge))
            scores = []
            for base, edge in blocks:
                qb = qd[pl.ds(base, ATT_BQ), :]
                kw = kd[pl.ds(base, ATT_W), :]
                for hd in range(2):
                    qh = jnp.where(head0 if hd == 0 else ~head0, qb, jnp.zeros_like(qb))
                    scores.append(lax.dot_general(qh, kw, (((1,), (1,)), ((), ())),
                                                  preferred_element_type=F32) + bias[edge])
            maxes = [jnp.max(s, axis=1, keepdims=True) for s in scores]
            probs = [jnp.exp(s - m).astype(BF16) for s, m in zip(scores, maxes)]
            fulls = []
            for i, (base, edge) in enumerate(blocks):
                vw = vd[pl.ds(base, ATT_W), :]
                for hd in range(2):
                    vh = jnp.where(khead0 if hd == 0 else ~khead0, vw, jnp.ones_like(vw))
                    fulls.append(jnp.dot(probs[2 * i + hd], vh, preferred_element_type=F32))
            for i, (base, edge) in enumerate(blocks):
                f0, f1 = fulls[2 * i], fulls[2 * i + 1]
                o_new = jnp.where(head0, f0, f1)
                l_new = pltpu.roll(jnp.where(head0, f1, f0), ATTN_HEAD_DIM, 1)
                l_new = jnp.maximum(l_new, TINY)
                m_new = jnp.where(head0, maxes[2 * i], maxes[2 * i + 1])
                rows = pl.ds(base, ATT_BQ)
                if mode == "first":
                    acc[rows, :] = o_new
                    ms[rows, :] = m_new
                    ls[rows, :] = l_new
                    continue
                m_old = ms[rows, :]
                m_tot = jnp.maximum(m_old, m_new)
                a_old = jnp.exp(m_old - m_tot)
                a_new = jnp.exp(m_new - m_tot)
                o_tot = acc[rows, :] * a_old + o_new * a_new
                l_tot = ls[rows, :] * a_old + l_new * a_new
                if mode == "merge":
                    acc[rows, :] = o_tot
                    ls[rows, :] = l_tot
                    ms[rows, :] = m_tot
                else:
                    o_ref[rows, :] = (o_tot / l_tot).astype(BF16)
            return c

        lax.fori_loop(0, seq // (ATT_BQ * nb), group, 0)

    branch(16, q16, k16, v16, st_a, "first")
    relayout_16_to_4(st_a, st_b)
    branch(4, q4, k4, v4, st_b, "merge")
    relayout_4_to_1(st_b, st_a)
    branch(1, q_ref, k1, v1, st_a, "last")


def _attention(proj, batch, seq):
    t = batch * seq
    padded = seq + 2 * ATTN_RADIUS
    qkv_spec = lambda off, mode=None: pl.BlockSpec(
        (None, seq, LANES), lambda b, hp: (off + hp, b, 0), pipeline_mode=mode)
    return pl.pallas_call(
        functools.partial(_attn_body, seq=seq),
        grid=(batch, ATTN_WIDTH // LANES),
        in_specs=[qkv_spec(0), qkv_spec(4), qkv_spec(8, pl.Buffered(1))],
        out_specs=pl.BlockSpec((seq, LANES), lambda b, hp: (b, hp)),
        out_shape=jax.ShapeDtypeStruct((t, ATTN_WIDTH), BF16),
        scratch_shapes=[pltpu.VMEM((padded, LANES), BF16), pltpu.VMEM((padded, LANES), BF16),
                        pltpu.VMEM((seq, LANES), BF16),
                        pltpu.VMEM((padded, LANES), BF16), pltpu.VMEM((padded, LANES), BF16),
                        pltpu.VMEM((seq, LANES), BF16),
                        pltpu.VMEM((padded, LANES), BF16), pltpu.VMEM((padded, LANES), BF16),
                        pltpu.VMEM((4, ATT_BQ, ATT_W), F32)]
        + [pltpu.VMEM((seq, LANES), F32)] * 6,
        compiler_params=_cparams(("arbitrary", "arbitrary"), ATTN_VMEM_LIMIT),
        name="dilated_attn",
    )(proj, proj, proj)


def _ret_body(lg_ref, q_ref, k_ref, v_ref, gate_ref, gain_ref, o_ref, ob, dec, *, seq):
    c_len = RET_CHUNK
    n_chunks = seq // c_len
    hd = pl.program_id(1)
    lf = lg_ref[0, hd]
    lb = lg_ref[1, hd]
    ri = lax.broadcasted_iota(jnp.int32, (c_len, c_len), 0).astype(F32)
    ci = lax.broadcasted_iota(jnp.int32, (c_len, c_len), 1).astype(F32)
    diff = ri - ci
    dec[0] = jnp.where(diff >= 0, jnp.exp(lf * jnp.maximum(diff, 0.0)),
                       jnp.exp(lb * jnp.maximum(-diff, 0.0)))
    dec[1] = jnp.exp(lf * (ri + 1.0))
    dec[2] = jnp.exp(lf * (c_len - 1.0 - ri))
    dec[3] = jnp.exp(lb * (c_len - ri))
    dec[4] = jnp.exp(lb * ri)
    cdf = jnp.exp(lf * c_len)
    cdb = jnp.exp(lb * c_len)
    tdot = (((0,), (0,)), ((), ()))
    unroll = min(n_chunks, RET_UNROLL)

    def bwd(jg, state):
        row_sets = [pl.ds(pl.multiple_of((n_chunks - 1 - (jg * unroll + u)) * c_len, c_len), c_len)
                    for u in range(unroll)]
        states = []
        for rows in row_sets:
            states.append(state)
            kn = k_ref[rows, :].astype(F32)
            kv = lax.dot_general((kn * dec[4]).astype(BF16), v_ref[rows, :], tdot,
                                 preferred_element_type=F32)
            state = state * cdb + kv
        for rows, st in zip(row_sets, states):
            qn = q_ref[rows, :].astype(F32)
            ob[rows, :] = jnp.dot((qn * dec[3]).astype(BF16), st.astype(BF16),
                                  preferred_element_type=F32)
        return state

    lax.fori_loop(0, n_chunks // unroll, bwd, jnp.zeros((c_len, c_len), F32))

    gain = gain_ref[...]

    def fwd(jg, state):
        row_sets = [pl.ds(pl.multiple_of((jg * unroll + u) * c_len, c_len), c_len)
                    for u in range(unroll)]
        states = []
        for rows in row_sets:
            states.append(state)
            kn = k_ref[rows, :].astype(F32)
            kv = lax.dot_general((kn * dec[2]).astype(BF16), v_ref[rows, :], tdot,
                                 preferred_element_type=F32)
            state = state * cdf + kv
        for rows, st in zip(row_sets, states):
            qb = q_ref[rows, :]
            vb = v_ref[rows, :]
            sc = lax.dot_general(qb, k_ref[rows, :], (((1,), (1,)), ((), ())),
                                 preferred_element_type=F32) * dec[0]
            tot = (ob[rows, :] + jnp.dot(sc.astype(BF16), vb, preferred_element_type=F32)
                   + jnp.dot((qb.astype(F32) * dec[1]).astype(BF16), st.astype(BF16),
                             preferred_element_type=F32))
            y = tot - jnp.mean(tot, axis=-1, keepdims=True)
            y = y * lax.rsqrt(jnp.mean(y * y, axis=-1, keepdims=True) + NORM_EPS) * gain
            o_ref[rows, :] = (y * _silu(gate_ref[rows, :].astype(F32))).astype(BF16)
        return state

    lax.fori_loop(0, n_chunks // unroll, fwd, jnp.zeros((c_len, c_len), F32))


def _retention(proj, gain, lg, batch, seq):
    t = batch * seq
    spec = lambda off: pl.BlockSpec((None, seq, LANES), lambda b, hd: (off + hd, b, 0))
    return pl.pallas_call(
        functools.partial(_ret_body, seq=seq),
        grid=(batch, N_RET_HEADS),
        in_specs=[pl.BlockSpec(memory_space=pltpu.SMEM),
                  spec(12), spec(16), spec(20), spec(24),
                  pl.BlockSpec((None, 1, LANES), lambda b, hd: (hd, 0, 0))],
        out_specs=pl.BlockSpec((seq, LANES), lambda b, hd: (b, hd)),
        out_shape=jax.ShapeDtypeStruct((t, RET_WIDTH), BF16),
        scratch_shapes=[pltpu.VMEM((seq, LANES), F32),
                        pltpu.VMEM((5, RET_CHUNK, RET_CHUNK), F32)],
        compiler_params=_cparams(("arbitrary", "arbitrary")),
        name="retention",
    )(lg, proj, proj, proj, proj, gain)


def _out_proj(x_ref, a_ref, r_ref, wo_ref, ag_ref, hm_ref, rows=slice(None)):
    a = a_ref[rows, :].astype(F32)
    mean = jnp.dot((a * a).astype(BF16), hm_ref[...], preferred_element_type=F32)
    an = (a * lax.rsqrt(mean + NORM_EPS) * ag_ref[...]).astype(BF16)
    return (x_ref[rows, :]
            + jnp.dot(an, wo_ref[0:ATTN_WIDTH, :], preferred_element_type=F32)
            + jnp.dot(r_ref[rows, :], wo_ref[ATTN_WIDTH:, :], preferred_element_type=F32))


def _post_dense_body(x_ref, a_ref, r_ref, wo_ref, ag_ref, hm_ref, g_ref, w1_ref, w3_ref, w2_ref,
                     fg_ref, o_ref, *, final_norm):
    x2 = _out_proj(x_ref, a_ref, r_ref, wo_ref, ag_ref, hm_ref)
    h = _rms(x2, g_ref[...]).astype(BF16)
    ffn = None
    for c0, c1 in FF_CHUNKS:
        a = jnp.dot(h, w1_ref[:, c0:c1], preferred_element_type=F32)
        b = jnp.dot(h, w3_ref[:, c0:c1], preferred_element_type=F32)
        u = (_silu(a) * b).astype(BF16)
        d = jnp.dot(u, w2_ref[c0:c1, :], preferred_element_type=F32)
        ffn = d if ffn is None else ffn + d
    y = x2 + ffn
    if final_norm:
        y = _rms(y, fg_ref[...])
    o_ref[...] = y


def _post_dense(x2d, att, ret, wo, ag, hm, g, w1, w3, w2, fg, final_norm):
    t = x2d.shape[0]
    full = lambda a: pl.BlockSpec(a.shape, lambda i: (0,) * a.ndim)
    return pl.pallas_call(
        functools.partial(_post_dense_body, final_norm=final_norm),
        grid=(t // TM,),
        in_specs=[pl.BlockSpec((TM, D_MODEL), lambda i: (i, 0)),
                  pl.BlockSpec((TM, ATTN_WIDTH), lambda i: (i, 0)),
                  pl.BlockSpec((TM, RET_WIDTH), lambda i: (i, 0)),
                  full(wo), full(ag), full(hm), full(g), full(w1), full(w3), full(w2), full(fg)],
        out_specs=pl.BlockSpec((TM, D_MODEL), lambda i: (i, 0)),
        out_shape=jax.ShapeDtypeStruct((t, D_MODEL), F32),
        compiler_params=_cparams(("arbitrary",)),
        name="post_dense",
    )(x2d, att, ret, wo, ag, hm, g, w1, w3, w2, fg)


def _post_route_body(x_ref, a_ref, r_ref, wo_ref, ag_ref, hm_ref, g_ref, rt_ref,
                     x2_ref, h_ref, ri_ref, rf_ref, cnt_ref, base):
    @pl.when(pl.program_id(0) == 0)
    def _():
        base[...] = jnp.zeros_like(base)

    part = TM // ROUTE_PARTS
    parts = [pl.ds(k * part, part) for k in range(ROUTE_PARTS)]
    x2s = [_out_proj(x_ref, a_ref, r_ref, wo_ref, ag_ref, hm_ref, rows) for rows in parts]
    hs = [_rms(x2, g_ref[...]) for x2 in x2s]
    for rows, x2, h in zip(parts, x2s, hs):
        x2_ref[rows, :] = x2
        h_ref[rows, :] = h
    logits = []
    for h in hs:
        hi = h.astype(BF16)
        lo = (h - hi.astype(F32)).astype(BF16)
        pieces = (jnp.dot(hi, rt_ref[...], preferred_element_type=F32)
                  + jnp.dot(lo, rt_ref[...], preferred_element_type=F32))
        logits.append(pieces + pltpu.roll(pieces, LANES - N_EXPERTS, 1))
    lane = lax.broadcasted_iota(jnp.int32, (part, LANES), 1).astype(F32)
    row = lax.broadcasted_iota(jnp.int32, (part, part), 0)
    col = lax.broadcasted_iota(jnp.int32, (part, part), 1)
    earlier = jnp.where(col < row, 1.0, 0.0).astype(BF16)
    routed = []
    for lgt in logits:
        lg = jnp.where(lane < N_EXPERTS, lgt, -jnp.inf)
        m1 = jnp.max(lg, axis=1, keepdims=True)
        i1 = jnp.min(jnp.where(lg == m1, lane, float(LANES)), axis=1, keepdims=True)
        lg2 = jnp.where(lane == i1, -jnp.inf, lg)
        m2 = jnp.max(lg2, axis=1, keepdims=True)
        i2 = jnp.min(jnp.where(lg2 == m2, lane, float(LANES)), axis=1, keepdims=True)
        e = jnp.exp(m2 - m1)
        sel1 = lane == i1
        sel2 = lane == i2
        onehot = jnp.where(sel1 | sel2, 1.0, 0.0)
        within = jnp.dot(earlier, onehot.astype(BF16), preferred_element_type=F32)
        routed.append((i1, i2, 1.0 / (1.0 + e), e / (1.0 + e), sel1, sel2, onehot, within))
    running = base[...]
    for rows, (i1, i2, g1, g2, sel1, sel2, onehot, within) in zip(parts, routed):
        cnt = within + running
        rank1 = jnp.sum(jnp.where(sel1, cnt, 0.0), axis=1, keepdims=True)
        rank2 = jnp.sum(jnp.where(sel2, cnt, 0.0), axis=1, keepdims=True)
        running = running + jnp.sum(onehot, axis=0, keepdims=True)
        ri = jnp.where(lane == 0, i1, jnp.where(lane == 1, i2,
                       jnp.where(lane == 2, rank1, jnp.where(lane == 3, rank2, 0.0))))
        ri_ref[:, rows] = jnp.transpose(ri)[0:8, :].astype(jnp.int32)
        rf_ref[rows, :] = jnp.where(lane == 0, g1, jnp.where(lane == 1, g2, 0.0))
    base[...] = running
    cnt_ref[...] = jnp.broadcast_to(running, cnt_ref.shape)


def _post_route(x2d, att, ret, wo, ag, hm, g, router_pad):
    t = x2d.shape[0]
    full = lambda a: pl.BlockSpec(a.shape, lambda i: (0,) * a.ndim)
    row = lambda w: pl.BlockSpec((TM, w), lambda i: (i, 0))
    return pl.pallas_call(
        _post_route_body,
        grid=(t // TM,),
        in_specs=[row(D_MODEL), row(ATTN_WIDTH), row(RET_WIDTH),
                  full(wo), full(ag), full(hm), full(g), full(router_pad)],
        out_specs=[row(D_MODEL), row(D_MODEL), pl.BlockSpec((8, TM), lambda i: (0, i)), row(LANES),
                   pl.BlockSpec((8, LANES), lambda i: (0, 0))],
        out_shape=[jax.ShapeDtypeStruct((t, D_MODEL), F32),
                   jax.ShapeDtypeStruct((t, D_MODEL), F32),
                   jax.ShapeDtypeStruct((8, t), jnp.int32),
                   jax.ShapeDtypeStruct((t, LANES), F32),
                   jax.ShapeDtypeStruct((8, LANES), F32)],
        scratch_shapes=[pltpu.VMEM((1, LANES), F32)],
        compiler_params=_cparams(("arbitrary",)),
        name="post_route",
    )(x2d, att, ret, wo, ag, hm, g, router_pad)


def _step_indices(pos_hbm, idx, sem_i):
    i = pl.program_id(0)
    slot = i % 2
    width = 2 * TMC

    def fetch(step, s):
        dst = idx.at[pl.ds(pl.multiple_of(s * width, width), width)]
        return pltpu.make_async_copy(pos_hbm.at[step], dst, sem_i.at[s])

    @pl.when(i == 0)
    def _():
        fetch(0, 0).start()

    fetch(i, slot).wait()

    @pl.when(i + 1 < pl.num_programs(0))
    def _():
        fetch(i + 1, 1 - slot).start()

    return pl.multiple_of(slot * width, width)


def _scatter_body(ginfo, pos_hbm, h_ref, o_hbm, idx, zbuf, sem_i, sem_r, sem_z, *, n_sorted):
    i = pl.program_id(0)

    @pl.when(i == 0)
    def _():
        zbuf[...] = jnp.zeros_like(zbuf)
        total = ginfo[N_EXPERTS - 1]
        fills = []
        for e in range(N_EXPERTS):
            start = pl.multiple_of(jnp.maximum(ginfo[e] - TME, 0), TME)
            fills.append((ginfo[N_EXPERTS + e] > 0, start))
        for j in range(N_EXPERTS):
            start = n_sorted - (j + 1) * TME
            fills.append((start >= total, start))
        for phase in ("start", "wait"):
            for cond, start in fills:
                @pl.when(cond)
                def _(start=start, phase=phase):
                    cp = pltpu.make_async_copy(zbuf, o_hbm.at[pl.ds(start, TME), :], sem_z)
                    cp.start() if phase == "start" else cp.wait()

    slot = _step_indices(pos_hbm, idx, sem_i)

    def copies(g, u):
        t = g * DMA_GROUP + u
        src = h_ref.at[g, pl.ds(u, 1), :]
        return (pltpu.make_async_copy(src, o_hbm.at[pl.ds(idx[slot + t], 1), :], sem_r),
                pltpu.make_async_copy(src, o_hbm.at[pl.ds(idx[slot + TMC + t], 1), :], sem_r))

    def issue(g, c):
        for u in range(DMA_GROUP):
            for cp in copies(g, u):
                cp.start()
        return c

    lax.fori_loop(0, TMC // DMA_GROUP, issue, 0)

    def drain(g, c):
        for u in range(DMA_GROUP):
            for cp in copies(g, u):
                cp.wait()
        return c

    lax.fori_loop(0, TMC // DMA_GROUP, drain, 0)


def _scatter(ginfo, pos, h, n_sorted):
    t = h.shape[0]
    return pl.pallas_call(
        functools.partial(_scatter_body, n_sorted=n_sorted),
        grid=(t // TMC,),
        in_specs=[pl.BlockSpec(memory_space=pltpu.SMEM),
                  pl.BlockSpec(memory_space=pl.ANY),
                  pl.BlockSpec((TMC // DMA_GROUP, DMA_GROUP, D_MODEL), lambda i: (i, 0, 0))],
        out_specs=pl.BlockSpec(memory_space=pl.ANY),
        out_shape=jax.ShapeDtypeStruct((n_sorted, D_MODEL), F32),
        scratch_shapes=[pltpu.SMEM((4 * TMC,), jnp.int32),
                        pltpu.VMEM((TME, D_MODEL), F32),
                        pltpu.SemaphoreType.DMA((2,)), pltpu.SemaphoreType.DMA,
                        pltpu.SemaphoreType.DMA],
        compiler_params=_cparams(("arbitrary",)),
        name="moe_dispatch",
    )(ginfo, pos, h.reshape(t // DMA_GROUP, DMA_GROUP, D_MODEL))


def _expert_body(te_ref, nv_ref, h_ref, w1_ref, w3_ref, w2_ref, o_ref):
    nv = nv_ref[pl.program_id(0)]

    @pl.when(nv == 0)
    def _():
        o_ref[...] = jnp.zeros_like(o_ref)

    @pl.when(nv > 0)
    def _():
        rows = lax.broadcasted_iota(jnp.int32, (TME, D_MODEL), 0)
        h = jnp.where(rows < nv, h_ref[...], 0.0).astype(BF16)
        ffn = None
        for c0, c1 in FF_CHUNKS:
            a = jnp.dot(h, w1_ref[:, c0:c1], preferred_element_type=F32)
            b = jnp.dot(h, w3_ref[:, c0:c1], preferred_element_type=F32)
            u = (_silu(a) * b).astype(BF16)
            d = jnp.dot(u, w2_ref[c0:c1, :], preferred_element_type=F32)
            ffn = d if ffn is None else ffn + d
        o_ref[...] = ffn


def _experts(tile_e, tile_nv, hs, w1, w3, w2):
    n_sorted = hs.shape[0]
    grid_spec = pltpu.PrefetchScalarGridSpec(
        num_scalar_prefetch=2,
        grid=(n_sorted // TME,),
        in_specs=[pl.BlockSpec((TME, D_MODEL), lambda j, te, nv: (j, 0)),
                  pl.BlockSpec((None, D_MODEL, D_FF), lambda j, te, nv: (te[j], 0, 0)),
                  pl.BlockSpec((None, D_MODEL, D_FF), lambda j, te, nv: (te[j], 0, 0)),
                  pl.BlockSpec((None, D_FF, D_MODEL), lambda j, te, nv: (te[j], 0, 0))],
        out_specs=pl.BlockSpec((TME, D_MODEL), lambda j, te, nv: (j, 0)),
    )
    return pl.pallas_call(
        _expert_body,
        grid_spec=grid_spec,
        out_shape=jax.ShapeDtypeStruct((n_sorted, D_MODEL), F32),
        compiler_params=_cparams(("arbitrary",)),
        name="moe_experts",
    )(tile_e, tile_nv, hs, w1, w3, w2)


def _combine_body(pos_hbm, x_ref, rf_ref, fg_ref, y_hbm, o_ref, idx, b1, b2, sem_i, sem_r,
                  *, final_norm):
    slot = _step_indices(pos_hbm, idx, sem_i)
    half_groups = TMC // DMA_GROUP // 2

    def copies(g, u, sem):
        t = g * DMA_GROUP + u
        return (pltpu.make_async_copy(y_hbm.at[pl.ds(idx[slot + t], 1), :],
                                      b1.at[g, pl.ds(u, 1), :], sem),
                pltpu.make_async_copy(y_hbm.at[pl.ds(idx[slot + TMC + t], 1), :],
                                      b2.at[g, pl.ds(u, 1), :], sem))

    def walk(half, start):
        def body(g, c):
            for u in range(DMA_GROUP):
                for cp in copies(half * half_groups + g, u, sem_r.at[half]):
                    cp.start() if start else cp.wait()
            return c
        lax.fori_loop(0, half_groups, body, 0)

    walk(0, True)
    walk(1, True)
    for half in range(2):
        walk(half, False)
        rows = pl.ds(half * (TMC // 2), TMC // 2)
        groups = pl.ds(half * half_groups, half_groups)
        g1 = rf_ref[rows, 0:1]
        g2 = rf_ref[rows, 1:2]
        y = x_ref[rows, :] + (g1 * b1[groups].reshape(TMC // 2, D_MODEL)
                              + g2 * b2[groups].reshape(TMC // 2, D_MODEL))
        if final_norm:
            y = _rms(y, fg_ref[...])
        o_ref[rows, :] = y


def _combine(pos, x2, rf, fg, ys, final_norm):
    t = x2.shape[0]
    return pl.pallas_call(
        functools.partial(_combine_body, final_norm=final_norm),
        grid=(t // TMC,),
        in_specs=[pl.BlockSpec(memory_space=pl.ANY),
                  pl.BlockSpec((TMC, D_MODEL), lambda i: (i, 0)),
                  pl.BlockSpec((TMC, LANES), lambda i: (i, 0)),
                  pl.BlockSpec((1, D_MODEL), lambda i: (0, 0)),
                  pl.BlockSpec(memory_space=pl.ANY)],
        out_specs=pl.BlockSpec((TMC, D_MODEL), lambda i: (i, 0)),
        out_shape=jax.ShapeDtypeStruct((t, D_MODEL), F32),
        scratch_shapes=[pltpu.SMEM((4 * TMC,), jnp.int32),
                        pltpu.VMEM((TMC // DMA_GROUP, DMA_GROUP, D_MODEL), F32),
                        pltpu.VMEM((TMC // DMA_GROUP, DMA_GROUP, D_MODEL), F32),
                        pltpu.SemaphoreType.DMA((2,)), pltpu.SemaphoreType.DMA((2,))],
        compiler_params=_cparams(("arbitrary",)),
        name="moe_combine",
    )(pos, x2, rf, fg, ys)


def _moe(x2d, att, ret, wo, ag, hm, g, router_pad, w1, w3, w2, fg, final_norm):
    t = x2d.shape[0]
    n_sorted = 2 * t + N_EXPERTS * TME
    n_tiles = n_sorted // TME
    x2, h, ri, rf, cnt = _post_route(x2d, att, ret, wo, ag, hm, g, router_pad)

    counts = cnt[0, :N_EXPERTS].astype(jnp.int32)
    padded = ((counts + TME - 1) // TME) * TME
    ends = jnp.cumsum(padded)
    starts = ends - padded
    experts = jnp.arange(N_EXPERTS, dtype=jnp.int32)
    start_of = lambda e: jnp.sum(jnp.where(e[:, None] == experts[None, :], starts[None, :], 0), axis=1)
    pos1 = start_of(ri[0]) + ri[2]
    pos2 = start_of(ri[1]) + ri[3]
    pos = jnp.concatenate([pos1.reshape(t // TMC, TMC), pos2.reshape(t // TMC, TMC)], axis=1)
    tile_start = jnp.arange(n_tiles, dtype=jnp.int32) * TME
    tile_e = jnp.minimum(jnp.sum(tile_start[:, None] >= ends[None, :], axis=1), N_EXPERTS - 1)
    tile_e = tile_e.astype(jnp.int32)
    tile_nv = jnp.clip(starts[tile_e] + counts[tile_e] - tile_start, 0, TME)
    tile_nv = jnp.where(tile_start < ends[-1], tile_nv, 0).astype(jnp.int32)
    ginfo = jnp.concatenate([ends, padded]).astype(jnp.int32)

    hs = _scatter(ginfo, pos, h, n_sorted)
    ys = _experts(tile_e, tile_nv, hs, w1, w3, w2)
    return _combine(pos, x2, rf, fg, ys, final_norm)


def _rope_tables(seq):
    pos = jnp.arange(seq, dtype=F32)[:, None]
    lane = jnp.arange(LANES)
    j = lane % ATTN_HEAD_DIM
    half = ROPE_DIM // 2
    inv_a = ROPE_THETA ** (-jnp.arange(half, dtype=F32) * 2.0 / ROPE_DIM)
    ang = pos * inv_a[None, :]
    cos_a, sin_a = jnp.cos(ang), jnp.sin(ang)
    jc = jnp.where(j < half, j, j - half) % half
    ca = jnp.where(j[None, :] < ROPE_DIM, cos_a[:, jc], 1.0)
    sa1 = jnp.where(((j >= half) & (j < ROPE_DIM))[None, :], sin_a[:, jc], 0.0)
    sa2 = jnp.where((j < half)[None, :], -sin_a[:, jc], 0.0)
    half_r = RET_HEAD_DIM // 2
    inv_r = RET_THETA ** (-jnp.arange(half_r, dtype=F32) * 2.0 / RET_HEAD_DIM)
    ang_r = pos * inv_r[None, :]
    cos_r, sin_r = jnp.cos(ang_r), jnp.sin(ang_r)
    cr = cos_r[:, lane % half_r]
    sr = jnp.where((lane < half_r)[None, :], -sin_r[:, lane % half_r], sin_r[:, lane % half_r])
    return ca, sa1, sa2, cr, sr


def _pack_router(router):
    hi = router.astype(BF16)
    lo = (router - hi.astype(F32)).astype(BF16)
    packed = jnp.concatenate([hi, lo], axis=-1)
    return jnp.pad(packed, ((0, 0), (0, 0), (0, LANES - 2 * N_EXPERTS)))


def _trunk(x, w):
    batch, seq, _ = x.shape
    t = batch * seq
    assert seq % (16 * ATT_BQ) == 0 and seq % TM == 0 and t % TMC == 0
    tabs = _rope_tables(seq)
    x2d = x.reshape(t, D_MODEL)
    depth = w["w_in"].shape[0]
    head_of = jnp.arange(ATTN_WIDTH) // ATTN_HEAD_DIM
    head_mean = jnp.where(head_of[:, None] == head_of[None, :], 1.0 / ATTN_HEAD_DIM, 0.0).astype(BF16)
    for layer in range(depth):
        last = layer == depth - 1
        ag = w["attn_out_g"][layer][None, :]
        proj = _in_proj(x2d, w["attn_norm_g"][layer][None, :], w["w_in"][layer], tabs, seq)
        att = _attention(proj, batch, seq)
        ret = _retention(proj, w["ret_out_g"][layer].reshape(N_RET_HEADS, 1, LANES),
                         w["lg"][layer], batch, seq)
        i = layer // 2
        if layer % 2 == 0:
            x2d = _post_dense(x2d, att, ret, w["w_out"][layer], ag, head_mean,
                              w["ffn_norm_g"][layer][None, :],
                              w["dense_w1"][i], w["dense_w3"][i], w["dense_w2"][i],
                              w["final_norm_g"], last)
        else:
            x2d = _moe(x2d, att, ret, w["w_out"][layer], ag, head_mean,
                       w["ffn_norm_g"][layer][None, :],
                       w["router_pad"][i], w["moe_w1"][i], w["moe_w3"][i], w["moe_w2"][i],
                       w["final_norm_g"], last)
    return x2d.reshape(batch, seq, D_MODEL)


def kernel(x_prompt, x_sample, attn_norm_g, w_in, attn_out_g, ret_out_g, ret_decay_fwd, ret_decay_bwd, w_out, ffn_norm_g, dense_w1, dense_w3, dense_w2, moe_router, moe_w1, moe_w3, moe_w2, final_norm_g):
    lg_f = jnp.log1p(-jnp.exp(ret_decay_fwd.astype(F32)))
    lg_b = jnp.log1p(-jnp.exp(ret_decay_bwd.astype(F32)))
    w = {
        "attn_norm_g": attn_norm_g, "attn_out_g": attn_out_g, "ret_out_g": ret_out_g,
        "ffn_norm_g": ffn_norm_g, "final_norm_g": final_norm_g[None, :],
        "lg": jnp.stack([lg_f, lg_b], axis=1),
        "w_in": w_in.astype(BF16), "w_out": w_out.astype(BF16),
        "dense_w1": dense_w1.astype(BF16), "dense_w3": dense_w3.astype(BF16),
        "dense_w2": dense_w2.astype(BF16),
        "moe_w1": moe_w1.astype(BF16), "moe_w3": moe_w3.astype(BF16), "moe_w2": moe_w2.astype(BF16),
        "router_pad": _pack_router(moe_router.astype(F32)),
    }
    return _trunk(x_prompt, w), _trunk(x_sample, w)
```
